```python
import math
import jax, jax.numpy as jnp
from jax import lax
import numpy as np

D_MODEL = 1024
BATCH = 2
SEQ = 8192
DEPTH = 4
DEC_BATCH = 128
DEC_SEQ = 8
PAST_LEN = 8192
PAGE_SIZE = 128

D_FF = 2816
N_SSM_LAYERS = (DEPTH + 1) // 2
N_MLA_LAYERS = DEPTH // 2
GM_WIDTH = D_MODEL // 2
GM_HEADS = 4
GM_HEAD_DIM = GM_WIDTH // GM_HEADS
CHUNK = 128
S5_WIDTH = D_MODEL // 2
S5_CH = 16
S5_GROUPS = S5_WIDTH // S5_CH
S5_STATE = 64
MLA_HEADS = 8
Q_LORA = 384
KV_LORA = 256
NOPE_DIM = 128
ROPE_DIM = 64
V_DIM = 128
ROPE_BASE = 10000.0
Q_BLOCK = 128
EPS = 1e-6
ATTN_SCALE = (NOPE_DIM + ROPE_DIM) ** -0.5

kernel_name = 'hybrid_gmlp_s5_mla_macaron_step'


def rmsnorm(x, g):
    xf = x.astype(jnp.float32)
    y = xf * lax.rsqrt(jnp.mean(xf * xf, axis=-1, keepdims=True) + EPS)
    return (y * g.astype(jnp.float32)).astype(x.dtype)


def layernorm(x, g, b):
    xf = x.astype(jnp.float32)
    mu = jnp.mean(xf, axis=-1, keepdims=True)
    xc = xf - mu
    y = xc * lax.rsqrt(jnp.mean(xc * xc, axis=-1, keepdims=True) + EPS)
    return (y * g.astype(jnp.float32) + b.astype(jnp.float32)).astype(x.dtype)


def swiglu(h, w_gate, w_up, w_down):
    return (jax.nn.silu(h @ w_gate) * (h @ w_up)) @ w_down


def rope(x, pos):
    half = ROPE_DIM // 2
    inv = ROPE_BASE ** (-jnp.arange(half, dtype=jnp.float32) * (2.0 / ROPE_DIM))
    ang = pos.astype(jnp.float32)[:, None] * inv[None, :]
    shape = (ang.shape[0],) + (1,) * (x.ndim - 3) + (half,)
    cos = jnp.cos(ang).reshape(shape)
    sin = jnp.sin(ang).reshape(shape)
    xf = x.astype(jnp.float32)
    x1, x2 = xf[..., :half], xf[..., half:]
    return jnp.concatenate([x1 * cos - x2 * sin, x2 * cos + x1 * sin], axis=-1).astype(x.dtype)


def chunk_spatial_gate(v, w_s, b_s):
    bt, L = v.shape[0], v.shape[1]
    n = -(-L // CHUNK)
    pad = n * CHUNK - L
    vp = jnp.pad(v, ((0, 0), (0, pad), (0, 0), (0, 0))).reshape(bt, n, CHUNK, GM_HEADS, GM_HEAD_DIM)
    causal = jnp.tril(jnp.ones((CHUNK, CHUNK), dtype=bool))
    w = jnp.where(causal[None], w_s, 0.0).astype(v.dtype)
    s = jnp.einsum('hts,bnshd->bnthd', w, vp) + b_s.T[None, None, :, :, None]
    return s.reshape(bt, n * CHUNK, GM_HEADS, GM_HEAD_DIM)[:, :L]


def _ssm_combine(e1, e2):
    a1r, a1i, b1r, b1i = e1
    a2r, a2i, b2r, b2i = e2
    return (a1r * a2r - a1i * a2i,
            a1r * a2i + a1i * a2r,
            a2r * b1r - a2i * b1i + b2r,
            a2r * b1i + a2i * b1r + b2i)


def s5_mixer(u, h0r, h0i, lam_re, lam_im, log_dt, b_re, b_im, c_re, c_im, d, w_glu, b_glu):
    bt, L = u.shape[0], u.shape[1]
    f32 = jnp.float32
    uf = u.astype(f32).reshape(bt, L, S5_GROUPS, S5_CH)
    lr, li = lam_re.astype(f32), lam_im.astype(f32)
    dt = jnp.exp(log_dt.astype(f32))[:, None]
    mag = jnp.exp(lr * dt)
    ar, ai = mag * jnp.cos(li * dt), mag * jnp.sin(li * dt)
    er, ei = ar - 1.0, ai
    den = lr * lr + li * li
    qr = (er * lr + ei * li) / den
    qi = (ei * lr - er * li) / den
    br, bi = b_re.astype(f32), b_im.astype(f32)
    bbr = qr[..., None] * br - qi[..., None] * bi
    bbi = qr[..., None] * bi + qi[..., None] * br
    xr = jnp.einsum('blgc,gnc->blgn', uf, bbr)
    xi = jnp.einsum('blgc,gnc->blgn', uf, bbi)
    h0r, h0i = h0r.astype(f32), h0i.astype(f32)
    xr = xr.at[:, 0].add(ar * h0r - ai * h0i)
    xi = xi.at[:, 0].add(ar * h0i + ai * h0r)
    arb = jnp.broadcast_to(ar, xr.shape)
    aib = jnp.broadcast_to(ai, xr.shape)
    _, _, hr, hi = lax.associative_scan(_ssm_combine, (arb, aib, xr, xi), axis=1)
    y = (jnp.einsum('blgn,gcn->blgc', hr, c_re.astype(f32))
         - jnp.einsum('blgn,gcn->blgc', hi, c_im.astype(f32))
         + uf * d.astype(f32).reshape(S5_GROUPS, S5_CH))
    g = jax.nn.gelu(y.reshape(bt, L, S5_WIDTH))
    out = g * jax.nn.sigmoid(g @ w_glu.astype(f32) + b_glu.astype(f32))
    return out.astype(u.dtype), hr[:, -1].astype(u.dtype), hi[:, -1].astype(u.dtype)


def even_mixer(h, h0r, h0i, w_in, gm_g, gm_b, gm_w_s, gm_b_s,
               lam_re, lam_im, log_dt, b_re, b_im, c_re, c_im, d, w_glu, b_glu, w_out):
    bt, L = h.shape[0], h.shape[1]
    z = h @ w_in
    za = jax.nn.gelu(z[..., :2 * GM_WIDTH])
    u, v = za[..., :GM_WIDTH], za[..., GM_WIDTH:]
    v = layernorm(v, gm_g, gm_b)
    s = chunk_spatial_gate(v.reshape(bt, L, GM_HEADS, GM_HEAD_DIM), gm_w_s, gm_b_s).reshape(bt, L, GM_WIDTH)
    out_a = u * s
    out_b, hr, hi = s5_mixer(z[..., 2 * GM_WIDTH:], h0r, h0i, lam_re, lam_im, log_dt,
                             b_re, b_im, c_re, c_im, d, w_glu, b_glu)
    y = jnp.concatenate([out_a, out_b], axis=-1) @ w_out
    return y, v, hr, hi


def mla_project(h, w_a, g_cq, g_ckv, w_uq, g_qn, g_qr, g_kr, pos):
    bt, L = h.shape[0], h.shape[1]
    a = h @ w_a
    c_q = rmsnorm(a[..., :Q_LORA], g_cq)
    c_kv = rmsnorm(a[..., Q_LORA:Q_LORA + KV_LORA], g_ckv)
    k_r = a[..., Q_LORA + KV_LORA:]
    q = (c_q @ w_uq).reshape(bt, L, MLA_HEADS, NOPE_DIM + ROPE_DIM)
    qn = rmsnorm(q[..., :NOPE_DIM], g_qn)
    qr = rope(rmsnorm(q[..., NOPE_DIM:], g_qr), pos)
    kr = rope(rmsnorm(k_r, g_kr), pos)
    return qn, qr, c_kv, kr


def key_nope(c, w_uk, g_kn):
    return rmsnorm(jnp.einsum('...r,rhd->...hd', c, w_uk), g_kn)


def mla_attend(qn, qr, kn, kr, c, mask):
    s = (jnp.einsum('bqhd,bkhd->bhqk', qn, kn, preferred_element_type=jnp.float32)
         + jnp.einsum('bqhd,bkd->bhqk', qr, kr, preferred_element_type=jnp.float32)) * ATTN_SCALE
    s = jnp.where(mask, s, -jnp.inf)
    p = jax.nn.softmax(s, axis=-1).astype(c.dtype)
    return jnp.einsum('bhqk,bkr->bqhr', p, c)


def mla_prompt_attention(qn, qr, kn, kr, c):
    bt, L = qn.shape[0], qn.shape[1]
    nb = L // Q_BLOCK
    qn_b = qn.reshape(bt, nb, Q_BLOCK, MLA_HEADS, NOPE_DIM).swapaxes(0, 1)
    qr_b = qr.reshape(bt, nb, Q_BLOCK, MLA_HEADS, ROPE_DIM).swapaxes(0, 1)
    q_pos = jnp.arange(L, dtype=jnp.int32).reshape(nb, Q_BLOCK)
    k_pos = jnp.arange(L, dtype=jnp.int32)

    def block(args):
        qn_i, qr_i, qp = args
        return mla_attend(qn_i, qr_i, kn, kr, c, qp[:, None] >= k_pos[None, :])

    o = lax.map(block, (qn_b, qr_b, q_pos))
    return o.swapaxes(0, 1).reshape(bt, L, MLA_HEADS, KV_LORA)


def mla_sample_attention(qn, qr, c_new, kr_new, cache_ckv, cache_krope, layer, page_table, w_uk, g_kn):
    n_pages = page_table.shape[1]
    past = n_pages * cache_ckv.shape[2]
    ds = qn.shape[1]
    k_pos = jnp.arange(past + ds, dtype=jnp.int32)
    q_pos = past + jnp.arange(ds, dtype=jnp.int32)
    mask = q_pos[:, None] >= k_pos[None, :]

    def one_seq(args):
        qn_b, qr_b, pages, cn, krn = args
        c_all = jnp.concatenate([cache_ckv[layer, pages].reshape(past, KV_LORA), cn], axis=0)
        kr_all = jnp.concatenate([cache_krope[layer, pages].reshape(past, ROPE_DIM), krn], axis=0)
        kn = key_nope(c_all, w_uk, g_kn)
        return mla_attend(qn_b[None], qr_b[None], kn[None], kr_all[None], c_all[None], mask)[0]

    return lax.map(one_seq, (qn, qr, page_table, c_new, kr_new))


def mla_output(o_lat, w_uv, w_o):
    bt, L = o_lat.shape[0], o_lat.shape[1]
    v = jnp.einsum('blhr,rhv->blhv', o_lat, w_uv).reshape(bt, L, MLA_HEADS * V_DIM)
    return v @ w_o


def setup_inputs(seed: int = 0) -> dict:
    key = jax.random.key(seed)
    ks = iter(jax.random.split(key, 64))
    f32 = jnp.float32

    def nrm(shape, scale=1.0):
        return jax.random.normal(next(ks), shape, f32) * scale

    def gain(shape):
        return 1.0 + nrm(shape, 0.02)

    n_pages = PAST_LEN // PAGE_SIZE
    n_used = DEC_BATCH * n_pages
    n_pool = n_used + n_used // 4
    Ls, Lm = N_SSM_LAYERS, N_MLA_LAYERS
    page_table = jax.random.permutation(next(ks), n_pool)[:n_used].reshape(DEC_BATCH, n_pages).astype(jnp.int32)
    lam_im0 = jnp.pi * jnp.arange(S5_STATE, dtype=f32)
    return dict(
        x_prompt=nrm((BATCH, SEQ, D_MODEL)),
        x_sample=nrm((DEC_BATCH, DEC_SEQ, D_MODEL)),
        cache_ckv=nrm((Lm, n_pool, PAGE_SIZE, KV_LORA)),
        cache_krope=nrm((Lm, n_pool, PAGE_SIZE, ROPE_DIM)),
        state_s5_re=nrm((Ls, DEC_BATCH, S5_GROUPS, S5_STATE), 0.5),
        state_s5_im=nrm((Ls, DEC_BATCH, S5_GROUPS, S5_STATE), 0.5),
        page_table=page_table,
        norm_ffn1=gain((DEPTH, D_MODEL)),
        ffn1_w_gate=nrm((DEPTH, D_MODEL, D_FF), D_MODEL ** -0.5),
        ffn1_w_up=nrm((DEPTH, D_MODEL, D_FF), D_MODEL ** -0.5),
        ffn1_w_down=nrm((DEPTH, D_FF, D_MODEL), D_FF ** -0.5),
        norm_mix=gain((DEPTH, D_MODEL)),
        norm_ffn2=gain((DEPTH, D_MODEL)),
        ffn2_w_gate=nrm((DEPTH, D_MODEL, D_FF), D_MODEL ** -0.5),
        ffn2_w_up=nrm((DEPTH, D_MODEL, D_FF), D_MODEL ** -0.5),
        ffn2_w_down=nrm((DEPTH, D_FF, D_MODEL), D_FF ** -0.5),
        ab_w_in=nrm((Ls, D_MODEL, 2 * GM_WIDTH + S5_WIDTH), D_MODEL ** -0.5),
        gm_norm_g=gain((Ls, GM_WIDTH)),
        gm_norm_b=nrm((Ls, GM_WIDTH), 0.02),
        gm_w_s=nrm((Ls, GM_HEADS, CHUNK, CHUNK), 0.5 * CHUNK ** -0.5),
        gm_b_s=1.0 + nrm((Ls, GM_HEADS, CHUNK), 0.1),
        s5_lambda_re=-0.5 + nrm((Ls, S5_GROUPS, S5_STATE), 0.01),
        s5_lambda_im=lam_im0 + nrm((Ls, S5_GROUPS, S5_STATE), 0.01),
        s5_log_dt=jax.random.uniform(next(ks), (Ls, S5_GROUPS), f32, minval=math.log(1e-3), maxval=math.log(1e-1)),
        s5_b_re=nrm((Ls, S5_GROUPS, S5_STATE, S5_CH), S5_CH ** -0.5),
        s5_b_im=nrm((Ls, S5_GROUPS, S5_STATE, S5_CH), S5_CH ** -0.5),
        s5_c_re=nrm((Ls, S5_GROUPS, S5_CH, S5_STATE), S5_STATE ** -0.5),
        s5_c_im=nrm((Ls, S5_GROUPS, S5_CH, S5_STATE), S5_STATE ** -0.5),
        s5_d=nrm((Ls, S5_WIDTH)),
        s5_w_glu=nrm((Ls, S5_WIDTH, S5_WIDTH), S5_WIDTH ** -0.5),
        s5_b_glu=nrm((Ls, S5_WIDTH), 0.02),
        ab_w_out=nrm((Ls, GM_WIDTH + S5_WIDTH, D_MODEL), (GM_WIDTH + S5_WIDTH) ** -0.5),
        mla_w_a=nrm((Lm, D_MODEL, Q_LORA + KV_LORA + ROPE_DIM), D_MODEL ** -0.5),
        mla_g_cq=gain((Lm, Q_LORA)),
        mla_g_ckv=gain((Lm, KV_LORA)),
        mla_w_uq=nrm((Lm, Q_LORA, MLA_HEADS * (NOPE_DIM + ROPE_DIM)), Q_LORA ** -0.5),
        mla_g_qn=gain((Lm, NOPE_DIM)),
        mla_g_qr=gain((Lm, ROPE_DIM)),
        mla_g_kr=gain((Lm, ROPE_DIM)),
        mla_w_uk=nrm((Lm, KV_LORA, MLA_HEADS, NOPE_DIM), KV_LORA ** -0.5),
        mla_g_kn=gain((Lm, NOPE_DIM)),
        mla_w_uv=nrm((Lm, KV_LORA, MLA_HEADS, V_DIM), KV_LORA ** -0.5),
        mla_w_o=nrm((Lm, MLA_HEADS * V_DIM, D_MODEL), (MLA_HEADS * V_DIM) ** -0.5),
    )


def reference(x_prompt, x_sample, cache_ckv, cache_krope, state_s5_re, state_s5_im, page_table,
              norm_ffn1, ffn1_w_gate, ffn1_w_up, ffn1_w_down, norm_mix, norm_ffn2,
              ffn2_w_gate, ffn2_w_up, ffn2_w_down,
              ab_w_in, gm_norm_g, gm_norm_b, gm_w_s, gm_b_s,
              s5_lambda_re, s5_lambda_im, s5_log_dt, s5_b_re, s5_b_im, s5_c_re, s5_c_im,
              s5_d, s5_w_glu, s5_b_glu, ab_w_out,
              mla_w_a, mla_g_cq, mla_g_ckv, mla_w_uq, mla_g_qn, mla_g_qr, mla_g_kr,
              mla_w_uk, mla_g_kn, mla_w_uv, mla_w_o):
    xp, xs = x_prompt, x_sample
    past_len = page_table.shape[1] * cache_ckv.shape[2]
    pos_p = jnp.arange(xp.shape[1], dtype=jnp.int32)
    pos_s = past_len + jnp.arange(xs.shape[1], dtype=jnp.int32)
    ckv_p, kr_p_l, ckv_s, kr_s_l = [], [], [], []
    s5r_p, s5i_p, s5r_s, s5i_s, gmv_s = [], [], [], [], []
    for l in range(DEPTH):
        xp = xp + 0.5 * swiglu(rmsnorm(xp, norm_ffn1[l]), ffn1_w_gate[l], ffn1_w_up[l], ffn1_w_down[l])
        xs = xs + 0.5 * swiglu(rmsnorm(xs, norm_ffn1[l]), ffn1_w_gate[l], ffn1_w_up[l], ffn1_w_down[l])
        if l % 2 == 0:
            i = l // 2
            gm = (gm_norm_g[i], gm_norm_b[i], gm_w_s[i], gm_b_s[i])
            ssm = (s5_lambda_re[i], s5_lambda_im[i], s5_log_dt[i], s5_b_re[i], s5_b_im[i],
                   s5_c_re[i], s5_c_im[i], s5_d[i], s5_w_glu[i], s5_b_glu[i])
            h0 = jnp.zeros((xp.shape[0], S5_GROUPS, S5_STATE), xp.dtype)
            yp, _, hr_p, hi_p = even_mixer(rmsnorm(xp, norm_mix[l]), h0, h0, ab_w_in[i], *gm, *ssm, ab_w_out[i])
            ys, v_s, hr_s, hi_s = even_mixer(rmsnorm(xs, norm_mix[l]), state_s5_re[i], state_s5_im[i],
                                             ab_w_in[i], *gm, *ssm, ab_w_out[i])
            s5r_p.append(hr_p)
            s5i_p.append(hi_p)
            s5r_s.append(hr_s)
            s5i_s.append(hi_s)
            gmv_s.append(v_s)
        else:
            j = l // 2
            proj = (mla_w_a[j], mla_g_cq[j], mla_g_ckv[j], mla_w_uq[j], mla_g_qn[j], mla_g_qr[j], mla_g_kr[j])
            qn_p, qr_p, c_p, kr_p = mla_project(rmsnorm(xp, norm_mix[l]), *proj, pos_p)
            kn_p = key_nope(c_p, mla_w_uk[j], mla_g_kn[j])
            yp = mla_output(mla_prompt_attention(qn_p, qr_p, kn_p, kr_p, c_p), mla_w_uv[j], mla_w_o[j])
            qn_s, qr_s, c_s, kr_s = mla_project(rmsnorm(xs, norm_mix[l]), *proj, pos_s)
            o_s = mla_sample_attention(qn_s, qr_s, c_s, kr_s, cache_ckv, cache_krope, j, page_table,
                                       mla_w_uk[j], mla_g_kn[j])
            ys = mla_output(o_s, mla_w_uv[j], mla_w_o[j])
            ckv_p.append(c_p)
            kr_p_l.append(kr_p)
            ckv_s.append(c_s)
            kr_s_l.append(kr_s)
        xp = xp + yp
        xs = xs + ys
        xp = xp + 0.5 * swiglu(rmsnorm(xp, norm_ffn2[l]), ffn2_w_gate[l], ffn2_w_up[l], ffn2_w_down[l])
        xs = xs + 0.5 * swiglu(rmsnorm(xs, norm_ffn2[l]), ffn2_w_gate[l], ffn2_w_up[l], ffn2_w_down[l])
    return (xp, xs,
            jnp.stack(ckv_p), jnp.stack(kr_p_l), jnp.stack(ckv_s), jnp.stack(kr_s_l),
            jnp.stack(s5r_p), jnp.stack(s5i_p), jnp.stack(s5r_s), jnp.stack(s5i_s),
            jnp.stack(gmv_s))
```

```python
import functools
import math

import jax
import jax.numpy as jnp
import numpy as np
from jax import lax
from jax.experimental import pallas as pl
from jax.experimental.pallas import tpu as pltpu

F32 = jnp.float32
BF16 = jnp.bfloat16

EPS = 1e-6
GM_HEADS = 4
CHUNK = 128
S5_CH = 16
S5_STATE = 64
S5_QCHUNKS = 4
MLA_HEADS = 8
Q_LORA = 384
KV_LORA = 256
NOPE_DIM = 128
ROPE_DIM = 64
ROPE_HALF = ROPE_DIM // 2
V_DIM = 128
ROPE_BASE = 10000.0
ATTN_SCALE = (NOPE_DIM + ROPE_DIM) ** -0.5
HEAD_PAD = 256
LANES = 128
SUBLANES = 8
VMEM_LIMIT = 56 * 1024 * 1024


def _pick(n, prefs):
    for p in prefs:
        if n % p == 0:
            return p
    raise ValueError(f"no tile in {prefs} divides {n}")


def _params(*sem):
    return pltpu.CompilerParams(dimension_semantics=sem, vmem_limit_bytes=VMEM_LIMIT)


def _const_spec(shape):
    n = len(shape)
    return pl.BlockSpec(shape, lambda *_: (0,) * n)


def _rms(x, g):
    return x * lax.rsqrt(jnp.mean(x * x, axis=-1, keepdims=True) + EPS) * g


def _sigmoid(x):
    return 1.0 / (1.0 + jnp.exp(-x))


def _gelu_tanh(x):
    return 0.5 * x * (1.0 + jnp.tanh(math.sqrt(2.0 / math.pi) * (x + 0.044715 * (x * x * x))))


def _dot(a, b):
    return jnp.dot(a, b, preferred_element_type=F32)


def _dot_nt(a, b):
    return lax.dot_general(a, b, (((1,), (1,)), ((), ())), preferred_element_type=F32)


def _ffn_body(x_ref, g_ref, wg_ref, wu_ref, wd_ref, o_ref, a_scr, *, fchunk):
    x = x_ref[...]
    h = _rms(x, g_ref[...]).astype(BF16)
    d_ff = wg_ref.shape[1]
    for c in range(0, d_ff, fchunk):
        g = _dot(h, wg_ref[:, c:c + fchunk])
        u = _dot(h, wu_ref[:, c:c + fchunk])
        a_scr[:, c:c + fchunk] = (g * _sigmoid(g) * u).astype(BF16)
    o_ref[...] = x + 0.5 * _dot(a_scr[...], wd_ref[...])


def _ffn(x, g, wg, wu, wd):
    t, d = x.shape
    d_ff = wg.shape[1]
    tm = _pick(t, (512, 256, 128, 64, 8))
    return pl.pallas_call(
        functools.partial(_ffn_body, fchunk=256),
        out_shape=jax.ShapeDtypeStruct((t, d), F32),
        grid=(t // tm,),
        in_specs=[
            pl.BlockSpec((tm, d), lambda i: (i, 0)),
            _const_spec((1, d)),
            _const_spec((d, d_ff)),
            _const_spec((d, d_ff)),
            _const_spec((d_ff, d)),
        ],
        out_specs=pl.BlockSpec((tm, d), lambda i: (i, 0)),
        scratch_shapes=[pltpu.VMEM((tm, d_ff), BF16)],
        compiler_params=_params("arbitrary"),
        name="ffn",
    )(x, g.reshape(1, d), wg, wu, wd)


def _s5_tables(lam_re, lam_im, log_dt, b_re, b_im, c_re, c_im):
    lr, li = lam_re.astype(F32), lam_im.astype(F32)
    dt = jnp.exp(log_dt.astype(F32))[:, None]
    mag = jnp.exp(lr * dt)
    ar, ai = mag * jnp.cos(li * dt), mag * jnp.sin(li * dt)
    er, ei = ar - 1.0, ai
    den = lr * lr + li * li
    qr = (er * lr + ei * li) / den
    qi = (ei * lr - er * li) / den
    br, bi = b_re.astype(F32), b_im.astype(F32)
    bbr = qr[..., None] * br - qi[..., None] * bi
    bbi = qr[..., None] * bi + qi[..., None] * br
    g = lr.shape[0]
    gq = g // S5_QCHUNKS
    eye = jnp.eye(gq, dtype=F32)

    def b_blk(bb):
        bq = bb.reshape(S5_QCHUNKS, gq, S5_STATE, S5_CH)
        return jnp.einsum('qgnc,gh->qhcgn', bq, eye).reshape(S5_QCHUNKS, gq * S5_CH, gq * S5_STATE)

    def c_blk(cc):
        cq = cc.astype(F32).reshape(S5_QCHUNKS, gq, S5_CH, S5_STATE)
        return jnp.einsum('qgcn,gh->qgnhc', cq, eye).reshape(S5_QCHUNKS, gq * S5_STATE, gq * S5_CH)

    bblk = jnp.concatenate([b_blk(bbr), b_blk(bbi)], axis=2).astype(BF16)
    cblk = jnp.concatenate([c_blk(c_re), -c_blk(c_im)], axis=1).astype(BF16)
    a_tab = jnp.stack([ar.reshape(S5_QCHUNKS, -1), ai.reshape(S5_QCHUNKS, -1)], axis=1)
    return a_tab, bblk, cblk


def _pack_state(hr, hi):
    n = hr.shape[0]
    r = hr.astype(F32).reshape(n, S5_QCHUNKS, 1, -1)
    i = hi.astype(F32).reshape(n, S5_QCHUNKS, 1, -1)
    return jnp.concatenate([r, i], axis=2).reshape(n, -1)


def _unpack_state(st, groups):
    n = st.shape[0]
    s4 = st.reshape(n, S5_QCHUNKS, 2, -1)
    return (s4[:, :, 0].reshape(n, groups, S5_STATE), s4[:, :, 1].reshape(n, groups, S5_STATE))


def _cmul(ar, ai, br, bi):
    return ar * br - ai * bi, ar * bi + ai * br


def _even_prompt_body(x_ref, gmix_ref, win_ref, lng_ref, lnb_ref, wtril_ref, sbias_ref,
                      atab_ref, bblk_ref, cblk_ref, d_ref, wglu_ref, bglu_ref, wout_ref,
                      o_ref, st_ref,
                      us_scr, usp_scr, xs_scr, yp_scr, y_scr, cat_scr, carry_scr, *, tt):
    m = tt // SUBLANES
    gw = lng_ref.shape[1]
    sw = d_ref.shape[1]
    hs = sw
    t_idx = pl.program_id(1)

    @pl.when(t_idx == 0)
    def _():
        carry_scr[...] = jnp.zeros_like(carry_scr)

    x = x_ref[0]
    hn = _rms(x, gmix_ref[...]).astype(BF16)
    z = _dot(hn, win_ref[...])
    u = _gelu_tanh(z[:, :gw])
    v = _gelu_tanh(z[:, gw:2 * gw])
    mu = jnp.mean(v, axis=-1, keepdims=True)
    vc = v - mu
    v = vc * lax.rsqrt(jnp.mean(vc * vc, axis=-1, keepdims=True) + EPS) * lng_ref[...] + lnb_ref[...]
    vb = v.astype(BF16)
    hd = gw // GM_HEADS
    for c in range(tt // CHUNK):
        rows = slice(c * CHUNK, (c + 1) * CHUNK)
        for h in range(GM_HEADS):
            cols = slice(h * hd, (h + 1) * hd)
            s = _dot(wtril_ref[h], vb[rows, cols]) + sbias_ref[:, cols]
            cat_scr[rows, cols] = (u[rows, cols] * s).astype(BF16)

    qw = sw // S5_QCHUNKS
    for q in range(S5_QCHUNKS):
        us_scr[q] = z[:, 2 * gw + q * qw:2 * gw + (q + 1) * qw]
    sub = lax.broadcasted_iota(jnp.int32, (SUBLANES, hs), 0)

    def shift_rows(val, k):
        return jnp.where(sub >= k, pltpu.roll(val, k, axis=0), 0.0)

    for q in range(S5_QCHUNKS):
        for i in range(m):
            usp_scr[i * SUBLANES:(i + 1) * SUBLANES, :] = us_scr.at[q][pl.ds(i, SUBLANES, stride=m), :]
        xs_scr[...] = _dot(usp_scr[...].astype(BF16), bblk_ref[q])
        ar = atab_ref[q, 0:1, :]
        ai = atab_ref[q, 1:2, :]
        arb = jnp.broadcast_to(ar, (SUBLANES, hs))
        aib = jnp.broadcast_to(ai, (SUBLANES, hs))

        def step(i, carry, store):
            hr, hi = carry
            r0 = pl.multiple_of(i * SUBLANES, SUBLANES)
            xr = xs_scr[pl.ds(r0, SUBLANES), 0:hs]
            xi = xs_scr[pl.ds(r0, SUBLANES), hs:2 * hs]
            nr = arb * hr - aib * hi + xr
            ni = arb * hi + aib * hr + xi
            if store:
                xs_scr[pl.ds(r0, SUBLANES), 0:hs] = nr
                xs_scr[pl.ds(r0, SUBLANES), hs:2 * hs] = ni
            return nr, ni

        zero = jnp.zeros((SUBLANES, hs), F32)
        er, ei = lax.fori_loop(0, m, functools.partial(step, store=False), (zero, zero), unroll=8)

        pr, pi = ar, ai
        for _ in range(int(math.log2(m))):
            pr, pi = _cmul(pr, pi, pr, pi)
        a1 = (pr, pi)
        a2 = _cmul(*a1, *a1)
        a4 = _cmul(*a2, *a2)
        fr, fi = er, ei
        for k, (kr, ki) in ((1, a1), (2, a2), (4, a4)):
            sr, si = shift_rows(fr, k), shift_rows(fi, k)
            dr, di = _cmul(kr, ki, sr, si)
            fr, fi = fr + dr, fi + di
        wr, wi = jnp.ones((SUBLANES, hs), F32), jnp.zeros((SUBLANES, hs), F32)
        for bit, (kr, ki) in ((1, a1), (2, a2), (4, a4)):
            sel = (sub & bit) != 0
            wr, wi = _cmul(wr, wi, jnp.where(sel, kr, 1.0), jnp.where(sel, ki, 0.0))
        cr = carry_scr[q, SUBLANES - 1:SUBLANES, 0:hs]
        ci = carry_scr[q, SUBLANES - 1:SUBLANES, hs:2 * hs]
        tr, ti = _cmul(wr, wi, cr, ci)
        h0r = tr + shift_rows(fr, 1)
        h0i = ti + shift_rows(fi, 1)
        nr, ni = _cmul(a1[0], a1[1], h0r, h0i)
        carry_scr[q, :, 0:hs] = nr + er
        carry_scr[q, :, hs:2 * hs] = ni + ei
        st_ref[0, :, q * 2 * hs:(q + 1) * 2 * hs] = carry_scr[q, SUBLANES - 1:SUBLANES, :]

        lax.fori_loop(0, m, functools.partial(step, store=True), (h0r, h0i), unroll=8)
        yp_scr[...] = _dot(xs_scr[...].astype(BF16), cblk_ref[q])
        for s in range(SUBLANES):
            y_scr[s * m:(s + 1) * m, q * qw:(q + 1) * qw] = (
                yp_scr[pl.ds(s, m, stride=SUBLANES), :]
                + d_ref[:, q * qw:(q + 1) * qw] * us_scr[q, s * m:(s + 1) * m, :])

    g = _gelu_tanh(y_scr[...])
    gate = _sigmoid(_dot(g.astype(BF16), wglu_ref[...]) + bglu_ref[...])
    cat_scr[:, gw:gw + sw] = (g * gate).astype(BF16)
    o_ref[0] = x + _dot(cat_scr[...], wout_ref[...])


def _even_prompt(xp, gmix, win, lng, lnb, wtril, sbias, atab, bblk, cblk, dvec, wglu, bglu, wout):
    b, l, d = xp.shape
    gw, sw = lng.shape[1], dvec.shape[1]
    tt = _pick(l, (256, 128))
    hs2 = 2 * sw
    args = (xp, gmix, win, lng, lnb, wtril, sbias, atab, bblk, cblk, dvec, wglu, bglu, wout)
    in_specs = [pl.BlockSpec((1, tt, d), lambda bi, ti: (bi, ti, 0))]
    in_specs += [_const_spec(a.shape) for a in args[1:]]
    return pl.pallas_call(
        functools.partial(_even_prompt_body, tt=tt),
        out_shape=(jax.ShapeDtypeStruct((b, l, d), F32),
                   jax.ShapeDtypeStruct((b, 1, S5_QCHUNKS * hs2), F32)),
        grid=(b, l // tt),
        in_specs=in_specs,
        out_specs=(pl.BlockSpec((1, tt, d), lambda bi, ti: (bi, ti, 0)),
                   pl.BlockSpec((1, 1, S5_QCHUNKS * hs2), lambda bi, ti: (bi, 0, 0))),
        scratch_shapes=[
            pltpu.VMEM((S5_QCHUNKS, tt, sw // S5_QCHUNKS), F32),
            pltpu.VMEM((tt, sw // S5_QCHUNKS), F32),
            pltpu.VMEM((tt, hs2), F32),
            pltpu.VMEM((tt, sw // S5_QCHUNKS), F32),
            pltpu.VMEM((tt, sw), F32),
            pltpu.VMEM((tt, gw + sw), BF16),
            pltpu.VMEM((S5_QCHUNKS, SUBLANES, hs2), F32),
        ],
        compiler_params=_params("arbitrary", "arbitrary"),
        name="even_prompt",
    )(*args)


def _even_sample_body(x_ref, h0_ref, gmix_ref, win_ref, lng_ref, lnb_ref, scoef_ref, sbias_ref,
                      atab_ref, bblk_ref, cblk_ref, d_ref, wglu_ref, bglu_ref, wout_ref,
                      o_ref, v_ref, st_ref,
                      xs_scr, y_scr, cat_scr, *, ds, nb):
    gw = lng_ref.shape[1]
    sw = d_ref.shape[1]
    hs = sw
    x = x_ref[...]
    hn = _rms(x, gmix_ref[...]).astype(BF16)
    z = _dot(hn, win_ref[...])
    u = _gelu_tanh(z[:, :gw])
    v = _gelu_tanh(z[:, gw:2 * gw])
    mu = jnp.mean(v, axis=-1, keepdims=True)
    vc = v - mu
    v = vc * lax.rsqrt(jnp.mean(vc * vc, axis=-1, keepdims=True) + EPS) * lng_ref[...] + lnb_ref[...]
    v_ref[...] = v
    for t in range(ds):
        s = jnp.broadcast_to(sbias_ref[t:t + 1, :], (nb, gw))
        for sp in range(t + 1):
            s = s + scoef_ref[t * ds + sp:t * ds + sp + 1, :] * v[sp * nb:(sp + 1) * nb, :]
        cat_scr[t * nb:(t + 1) * nb, 0:gw] = (u[t * nb:(t + 1) * nb, :] * s).astype(BF16)

    us = z[:, 2 * gw:]
    usb = us.astype(BF16)
    qw = sw // S5_QCHUNKS
    for q in range(S5_QCHUNKS):
        xs_scr[...] = _dot(usb[:, q * qw:(q + 1) * qw], bblk_ref[q])
        ar = atab_ref[q, 0:1, :]
        ai = atab_ref[q, 1:2, :]
        hr = h0_ref[:, q * 2 * hs:q * 2 * hs + hs]
        hi = h0_ref[:, q * 2 * hs + hs:(q + 1) * 2 * hs]
        for t in range(ds):
            rows = slice(t * nb, (t + 1) * nb)
            nr = ar * hr - ai * hi + xs_scr[rows, 0:hs]
            ni = ar * hi + ai * hr + xs_scr[rows, hs:2 * hs]
            xs_scr[rows, 0:hs] = nr
            xs_scr[rows, hs:2 * hs] = ni
            hr, hi = nr, ni
        st_ref[:, q * 2 * hs:q * 2 * hs + hs] = hr
        st_ref[:, q * 2 * hs + hs:(q + 1) * 2 * hs] = hi
        y_scr[:, q * qw:(q + 1) * qw] = _dot(xs_scr[...].astype(BF16), cblk_ref[q])
    g = _gelu_tanh(y_scr[...] + d_ref[...] * us)
    gate = _sigmoid(_dot(g.astype(BF16), wglu_ref[...]) + bglu_ref[...])
    cat_scr[:, gw:gw + sw] = (g * gate).astype(BF16)
    o_ref[...] = x + _dot(cat_scr[...], wout_ref[...])


def _even_sample(xs_tm, h0, gmix, win, lng, lnb, scoef, sbias8, atab, bblk, cblk, dvec, wglu, bglu, wout, *, ds):
    t, d = xs_tm.shape
    nb = t // ds
    gw, sw = lng.shape[1], dvec.shape[1]
    args = (xs_tm, h0, gmix, win, lng, lnb, scoef, sbias8, atab, bblk, cblk, dvec, wglu, bglu, wout)
    return pl.pallas_call(
        functools.partial(_even_sample_body, ds=ds, nb=nb),
        out_shape=(jax.ShapeDtypeStruct((t, d), F32),
                   jax.ShapeDtypeStruct((t, gw), F32),
                   jax.ShapeDtypeStruct(h0.shape, F32)),
        grid=(1,),
        in_specs=[_const_spec(a.shape) for a in args],
        out_specs=(_const_spec((t, d)), _const_spec((t, gw)), _const_spec(h0.shape)),
        scratch_shapes=[
            pltpu.VMEM((t, 2 * sw), F32),
            pltpu.VMEM((t, sw), F32),
            pltpu.VMEM((t, gw + sw), BF16),
        ],
        compiler_params=_params("arbitrary"),
        name="even_sample",
    )(*args)


def _mla_proj_body(x_ref, cos_ref, sin_ref, gmix_ref, wa_ref, gcq_ref, gckv_ref, gkr_ref,
                   wuq_ref, gqn_ref, gq1_ref, gq2_ref, wuk_ref, gkn_ref, ones_ref, perm_ref,
                   ckv_ref, kr_ref, cbf_ref, qcat_ref, kcat_ref):
    tm = x_ref.shape[0]
    x = x_ref[...]
    hn = _rms(x, gmix_ref[...]).astype(BF16)
    a = _dot(hn, wa_ref[...])
    cq = _rms(a[:, :Q_LORA], gcq_ref[...])
    ckv = _rms(a[:, Q_LORA:Q_LORA + KV_LORA], gckv_ref[...])
    ckv_ref[...] = ckv
    ckv_b = ckv.astype(BF16)
    cbf_ref[...] = ckv_b

    cos = cos_ref[...]
    sin = sin_ref[...]
    slab = a[:, Q_LORA + KV_LORA:]
    ms = jnp.sum(slab * slab, axis=-1, keepdims=True) * (1.0 / ROPE_DIM)
    kn = slab * lax.rsqrt(ms + EPS) * gkr_ref[...]
    lane = lax.broadcasted_iota(jnp.int32, (tm, LANES), 1)
    rot = jnp.where(lane < ROPE_HALF, -pltpu.roll(kn, LANES - ROPE_HALF, axis=1),
                    pltpu.roll(kn, ROPE_HALF, axis=1))
    kr = kn * cos + rot * sin
    kr_ref[...] = kr[:, :ROPE_DIM]
    kr_pad = jnp.where(lane < ROPE_DIM, kr, 0.0).astype(BF16)

    q = _dot(cq.astype(BF16), wuq_ref[...])
    k = _dot(ckv_b, wuk_ref[...])
    nn = MLA_HEADS * NOPE_DIM
    rw = MLA_HEADS * ROPE_HALF
    x1 = q[:, nn:nn + rw]
    x2 = q[:, nn + rw:nn + 2 * rw]
    ssq = x1 * x1 + x2 * x2
    ssq_hi = ssq.astype(BF16)
    ssq_lo = (ssq - ssq_hi.astype(F32)).astype(BF16)
    gsum = _dot(ssq_hi, ones_ref[...]) + _dot(ssq_lo, ones_ref[...])
    r = lax.rsqrt(gsum * (1.0 / ROPE_DIM) + EPS)
    x1n = x1 * r * gq1_ref[...]
    x2n = x2 * r * gq2_ref[...]
    reps = rw // LANES
    cos2 = jnp.concatenate([cos] * reps, axis=1)
    sin2 = jnp.concatenate([sin] * reps, axis=1)
    o1 = x1n * cos2 - x2n * sin2
    o2 = x2n * cos2 + x1n * sin2
    qr_cat = _dot(jnp.concatenate([o1, o2], axis=1).astype(BF16), perm_ref[...]).astype(BF16)
    for h in range(MLA_HEADS):
        nsl = slice(h * NOPE_DIM, (h + 1) * NOPE_DIM)
        qcat_ref[:, h * HEAD_PAD:h * HEAD_PAD + NOPE_DIM] = _rms(q[:, nsl], gqn_ref[...]).astype(BF16)
        qcat_ref[:, h * HEAD_PAD + NOPE_DIM:(h + 1) * HEAD_PAD] = qr_cat[:, h * LANES:(h + 1) * LANES]
        kcat_ref[:, h * HEAD_PAD:h * HEAD_PAD + NOPE_DIM] = _rms(k[:, nsl], gkn_ref[...]).astype(BF16)
        kcat_ref[:, h * HEAD_PAD + NOPE_DIM:(h + 1) * HEAD_PAD] = kr_pad


def _mla_proj(x, cos, sin, consts):
    t, d = x.shape
    tm = _pick(t, (256, 128, 64, 8))
    hw = MLA_HEADS * HEAD_PAD
    row = lambda w: pl.BlockSpec((tm, w), lambda i: (i, 0))
    return pl.pallas_call(
        _mla_proj_body,
        out_shape=(jax.ShapeDtypeStruct((t, KV_LORA), F32),
                   jax.ShapeDtypeStruct((t, ROPE_DIM), F32),
                   jax.ShapeDtypeStruct((t, KV_LORA), BF16),
                   jax.ShapeDtypeStruct((t, hw), BF16),
                   jax.ShapeDtypeStruct((t, hw), BF16)),
        grid=(t // tm,),
        in_specs=[row(d), row(LANES), row(LANES)] + [_const_spec(c.shape) for c in consts],
        out_specs=(row(KV_LORA), row(ROPE_DIM), row(KV_LORA), row(hw), row(hw)),
        compiler_params=_params("arbitrary"),
        name="mla_proj",
    )(x, cos, sin, *consts)


def _attn_prompt_body(qi_ref, ki_ref, q_ref, k_ref, c_ref, o_ref, m_scr, l_scr, acc_scr, *, tq):
    p = pl.program_id(1)
    qi = qi_ref[p]
    ki = ki_ref[p]

    @pl.when(ki == 0)
    def _():
        m_scr[...] = jnp.full_like(m_scr, -jnp.inf)
        l_scr[...] = jnp.zeros_like(l_scr)
        acc_scr[...] = jnp.zeros_like(acc_scr)

    def tile(masked):
        c = c_ref[0]
        if masked:
            row = lax.broadcasted_iota(jnp.int32, (tq, tq), 0)
            col = lax.broadcasted_iota(jnp.int32, (tq, tq), 1)
            keep = row >= col
        for h in range(MLA_HEADS):
            hsl = slice(h * HEAD_PAD, (h + 1) * HEAD_PAD)
            s = _dot_nt(q_ref[0, :, hsl], k_ref[0, :, hsl]) * ATTN_SCALE
            if masked:
                s = jnp.where(keep, s, -jnp.inf)
            m_prev = m_scr[h]
            m_new = jnp.maximum(m_prev, jnp.max(s, axis=-1, keepdims=True))
            alpha = jnp.exp(m_prev - m_new)
            pexp = jnp.exp(s - m_new)
            l_scr[h] = alpha * l_scr[h] + jnp.sum(pexp, axis=-1, keepdims=True)
            acc_scr[h] = alpha * acc_scr[h] + _dot(pexp.astype(BF16), c)
            m_scr[h] = m_new

    @pl.when(ki < qi)
    def _():
        tile(False)

    @pl.when(ki == qi)
    def _():
        tile(True)
        for h in range(MLA_HEADS):
            o_ref[0, :, h * KV_LORA:(h + 1) * KV_LORA] = (acc_scr[h] / l_scr[h]).astype(BF16)


def _attn_prompt(qcat, kcat, cbf):
    b, l, hw = qcat.shape
    tq = _pick(l, (512, 256, 128))
    nq = l // tq
    pairs = [(i, j) for i in range(nq) for j in range(i + 1)]
    qi_tab = jnp.asarray([i for i, _ in pairs], jnp.int32)
    ki_tab = jnp.asarray([j for _, j in pairs], jnp.int32)
    grid_spec = pltpu.PrefetchScalarGridSpec(
        num_scalar_prefetch=2,
        grid=(b, len(pairs)),
        in_specs=[
            pl.BlockSpec((1, tq, hw), lambda bi, p, qi, ki: (bi, qi[p], 0)),
            pl.BlockSpec((1, tq, hw), lambda bi, p, qi, ki: (bi, ki[p], 0)),
            pl.BlockSpec((1, tq, KV_LORA), lambda bi, p, qi, ki: (bi, ki[p], 0)),
        ],
        out_specs=pl.BlockSpec((1, tq, MLA_HEADS * KV_LORA), lambda bi, p, qi, ki: (bi, qi[p], 0)),
        scratch_shapes=[
            pltpu.VMEM((MLA_HEADS, tq, 1), F32),
            pltpu.VMEM((MLA_HEADS, tq, 1), F32),
            pltpu.VMEM((MLA_HEADS, tq, KV_LORA), F32),
        ],
    )
    return pl.pallas_call(
        functools.partial(_attn_prompt_body, tq=tq),
        out_shape=jax.ShapeDtypeStruct((b, l, MLA_HEADS * KV_LORA), BF16),
        grid_spec=grid_spec,
        compiler_params=_params("arbitrary", "arbitrary"),
        name="attn_prompt",
    )(qi_tab, ki_tab, qcat, kcat, cbf)


def _attn_sample_body(pt_ref, q_ref, cn_ref, krn_ref, wukt_ref, gkn_ref, ckv_hbm, krc_hbm,
                      o_ref, cbuf, krbuf, sems, lhs_scr, tail_c, tail_kr,
                      *, layer, n_pages, page, tk, ds):
    b = pl.program_id(0)
    nb = pl.num_programs(0)
    slot = lax.rem(b, 2)
    nh = MLA_HEADS
    nrow = nh * ds
    kw = nh * NOPE_DIM

    def page_copies(seq, sl, j):
        pg = pt_ref[seq, j]
        r0 = pl.multiple_of(j * page, page)
        return (pltpu.make_async_copy(ckv_hbm.at[layer, pg], cbuf.at[sl, pl.ds(r0, page), :], sems.at[0, sl]),
                pltpu.make_async_copy(krc_hbm.at[layer, pg], krbuf.at[sl, pl.ds(r0, page), :], sems.at[1, sl]))

    def start_fetch(seq, sl):
        def body(j, carry):
            for cp in page_copies(seq, sl, j):
                cp.start()
            return carry
        lax.fori_loop(0, n_pages, body, 0)

    @pl.when(b == 0)
    def _():
        start_fetch(0, 0)
        lhs_scr[0:kw, :] = wukt_ref[...]
        tail_c[...] = jnp.zeros_like(tail_c)
        tail_kr[...] = jnp.zeros_like(tail_kr)

    @pl.when(b + 1 < nb)
    def _():
        start_fetch(b + 1, 1 - slot)

    qf = q_ref[0]
    qabs, qrope = [], []
    for h in range(nh):
        qn = (qf[:, h * HEAD_PAD:h * HEAD_PAD + NOPE_DIM] * gkn_ref[...]).astype(BF16)
        qabs.append(_dot(qn, wukt_ref[h * NOPE_DIM:(h + 1) * NOPE_DIM, :]))
        qrope.append(qf[:, h * HEAD_PAD + NOPE_DIM:h * HEAD_PAD + NOPE_DIM + ROPE_DIM])
    lhs_scr[kw:kw + nrow, :] = jnp.concatenate(qabs, axis=0).astype(BF16)
    qr_b = jnp.concatenate(qrope, axis=0).astype(BF16)
    tail_c[0:ds, :] = cn_ref[0]
    tail_kr[0:ds, :] = krn_ref[0]

    def wait_body(j, carry):
        for cp in page_copies(b, slot, j):
            cp.wait()
        return carry
    lax.fori_loop(0, n_pages, wait_body, 0)

    def tile(c_f32, kr_f32, carry, width, masked):
        m_prev, l_prev, acc = carry
        cb = c_f32.astype(BF16)
        big = _dot_nt(lhs_scr[...], cb)
        rinv = []
        for h in range(nh):
            kt = big[h * NOPE_DIM:(h + 1) * NOPE_DIM, :]
            ss = jnp.sum(kt * kt, axis=0, keepdims=True) * (1.0 / NOPE_DIM)
            rinv.append(jnp.broadcast_to(lax.rsqrt(ss + EPS), (ds, width)))
        rinv = jnp.concatenate(rinv, axis=0)
        s = (big[kw:kw + nrow, :] * rinv + _dot_nt(qr_b, kr_f32.astype(BF16))) * ATTN_SCALE
        if masked:
            qpos = lax.rem(lax.broadcasted_iota(jnp.int32, (nrow, width), 0), ds)
            kpos = lax.broadcasted_iota(jnp.int32, (nrow, width), 1)
            s = jnp.where(kpos <= qpos, s, -jnp.inf)
        m_new = jnp.maximum(m_prev, jnp.max(s, axis=-1, keepdims=True))
        alpha = jnp.exp(m_prev - m_new)
        pexp = jnp.exp(s - m_new)
        l_new = alpha * l_prev + jnp.sum(pexp, axis=-1, keepdims=True)
        acc = alpha * acc + _dot(pexp.astype(BF16), cb)
        return m_new, l_new, acc

    def body(i, carry):
        r0 = pl.multiple_of(i * tk, tk)
        return tile(cbuf[slot, pl.ds(r0, tk), :], krbuf[slot, pl.ds(r0, tk), :], carry, tk, False)

    init = (jnp.full((nrow, 1), -jnp.inf, F32), jnp.zeros((nrow, 1), F32), jnp.zeros((nrow, KV_LORA), F32))
    carry = lax.fori_loop(0, (n_pages * page) // tk, body, init)
    _, l_fin, acc = tile(tail_c[...], tail_kr[...], carry, tail_c.shape[0], True)
    o_ref[0] = (acc / l_fin).astype(BF16)


def _attn_sample(page_table, q_s, c_new, kr_new, wukt, gkn, cache_ckv, cache_krope, *, layer):
    nb, ds, hw = q_s.shape
    n_pages = page_table.shape[1]
    page = cache_ckv.shape[2]
    tk = _pick(n_pages * page, (512, 256, 128))
    nrow = MLA_HEADS * ds
    kw = MLA_HEADS * NOPE_DIM
    grid_spec = pltpu.PrefetchScalarGridSpec(
        num_scalar_prefetch=1,
        grid=(nb,),
        in_specs=[
            pl.BlockSpec((1, ds, hw), lambda i, pt: (i, 0, 0)),
            pl.BlockSpec((1, ds, KV_LORA), lambda i, pt: (i, 0, 0)),
            pl.BlockSpec((1, ds, ROPE_DIM), lambda i, pt: (i, 0, 0)),
            pl.BlockSpec((kw, KV_LORA), lambda i, pt: (0, 0)),
            pl.BlockSpec((1, NOPE_DIM), lambda i, pt: (0, 0)),
            pl.BlockSpec(memory_space=pl.ANY),
            pl.BlockSpec(memory_space=pl.ANY),
        ],
        out_specs=pl.BlockSpec((1, nrow, KV_LORA), lambda i, pt: (i, 0, 0)),
        scratch_shapes=[
            pltpu.VMEM((2, n_pages * page, KV_LORA), F32),
            pltpu.VMEM((2, n_pages * page, ROPE_DIM), F32),
            pltpu.SemaphoreType.DMA((2, 2)),
            pltpu.VMEM((kw + nrow, KV_LORA), BF16),
            pltpu.VMEM((LANES, KV_LORA), F32),
            pltpu.VMEM((LANES, ROPE_DIM), F32),
        ],
    )
    return pl.pallas_call(
        functools.partial(_attn_sample_body, layer=layer, n_pages=n_pages, page=page, tk=tk, ds=ds),
        out_shape=jax.ShapeDtypeStruct((nb, nrow, KV_LORA), BF16),
        grid_spec=grid_spec,
        compiler_params=_params("arbitrary"),
        name="attn_sample",
    )(page_table, q_s, c_new, kr_new, wukt, gkn, cache_ckv, cache_krope)


def _mla_out_body(x_ref, o_ref_in, wuv_ref, wo_ref, out_ref, v_scr):
    for h in range(MLA_HEADS):
        v_scr[:, h * V_DIM:(h + 1) * V_DIM] = _dot(
            o_ref_in[:, h * KV_LORA:(h + 1) * KV_LORA], wuv_ref[h]).astype(BF16)
    out_ref[...] = x_ref[...] + _dot(v_scr[...], wo_ref[...])


def _mla_out(x, olat, wuv, wo):
    t, d = x.shape
    tm = _pick(t, (512, 256, 128, 64, 8))
    return pl.pallas_call(
        _mla_out_body,
        out_shape=jax.ShapeDtypeStruct((t, d), F32),
        grid=(t // tm,),
        in_specs=[pl.BlockSpec((tm, d), lambda i: (i, 0)),
                  pl.BlockSpec((tm, olat.shape[1]), lambda i: (i, 0)),
                  _const_spec(wuv.shape), _const_spec(wo.shape)],
        out_specs=pl.BlockSpec((tm, d), lambda i: (i, 0)),
        scratch_shapes=[pltpu.VMEM((tm, MLA_HEADS * V_DIM), BF16)],
        compiler_params=_params("arbitrary"),
        name="mla_out",
    )(x, olat, wuv, wo)


def _even_layer(x, b, l, nb, ds, h0r, h0i, gmix, w_in, gm_g, gm_b, gm_w_s, gm_b_s,
                lam_re, lam_im, log_dt, b_re, b_im, c_re, c_im, dvec, w_glu, b_glu, w_out):
    d = x.shape[1]
    gw = gm_g.shape[0]
    sw = dvec.shape[0]
    groups = lam_re.shape[0]
    atab, bblk, cblk = _s5_tables(lam_re, lam_im, log_dt, b_re, b_im, c_re, c_im)
    causal = jnp.tril(jnp.ones((CHUNK, CHUNK), dtype=bool))
    w_tril = jnp.where(causal[None], gm_w_s, 0.0)
    sbias = jnp.repeat(gm_b_s.T, gw // GM_HEADS, axis=1)
    scoef = jnp.repeat(w_tril[:, :ds, :ds].transpose(1, 2, 0).reshape(ds * ds, GM_HEADS),
                       gw // GM_HEADS, axis=1)
    common = (gmix.reshape(1, d), w_in.astype(BF16), gm_g.reshape(1, gw), gm_b.reshape(1, gw))
    tail = (atab, bblk, cblk, dvec.reshape(1, sw), w_glu.astype(BF16), b_glu.reshape(1, sw),
            w_out.astype(BF16))
    xp = x[:b * l].reshape(b, l, d)
    yp, st_p = _even_prompt(xp, *common, w_tril.astype(BF16), sbias, *tail)
    xs_tm = x[b * l:].reshape(nb, ds, d).transpose(1, 0, 2).reshape(ds * nb, d)
    ys_tm, v_tm, st_s = _even_sample(xs_tm, _pack_state(h0r, h0i), *common, scoef, sbias[:ds], *tail, ds=ds)
    ys = ys_tm.reshape(ds, nb, d).transpose(1, 0, 2).reshape(nb * ds, d)
    v_s = v_tm.reshape(ds, nb, gw).transpose(1, 0, 2)
    x_new = jnp.concatenate([yp.reshape(b * l, d), ys], axis=0)
    hr_p, hi_p = _unpack_state(st_p.reshape(b, -1), groups)
    hr_s, hi_s = _unpack_state(st_s, groups)
    return x_new, v_s, hr_p, hi_p, hr_s, hi_s


def _rope_tables(b, l, nb, ds, past):
    inv = ROPE_BASE ** (-jnp.arange(ROPE_HALF, dtype=F32) * (2.0 / ROPE_DIM))
    pos_p = jnp.arange(l, dtype=jnp.int32)
    pos_s = past + jnp.arange(ds, dtype=jnp.int32)
    pos = jnp.concatenate([jnp.tile(pos_p, b), jnp.tile(pos_s, nb)])
    ang = pos.astype(F32)[:, None] * inv[None, :]
    reps = LANES // ROPE_HALF
    return jnp.tile(jnp.cos(ang), (1, reps)), jnp.tile(jnp.sin(ang), (1, reps))


def _mla_layer(x, b, l, nb, ds, layer, cos, sin, gmix, cache_ckv, cache_krope, page_table,
               w_a, g_cq, g_ckv, w_uq, g_qn, g_qr, g_kr, w_uk, g_kn, w_uv, w_o):
    d = x.shape[1]
    nh = MLA_HEADS
    wa_pad = jnp.concatenate([w_a, jnp.zeros((d, LANES - ROPE_DIM), w_a.dtype)], axis=1).astype(BF16)
    wq3 = w_uq.reshape(Q_LORA, nh, NOPE_DIM + ROPE_DIM)
    wuq = jnp.concatenate([wq3[:, :, :NOPE_DIM].reshape(Q_LORA, -1),
                           wq3[:, :, NOPE_DIM:NOPE_DIM + ROPE_HALF].reshape(Q_LORA, -1),
                           wq3[:, :, NOPE_DIM + ROPE_HALF:].reshape(Q_LORA, -1)], axis=1).astype(BF16)
    wuk = w_uk.reshape(KV_LORA, nh * NOPE_DIM).astype(BF16)
    rw = nh * ROPE_HALF
    grp = np.arange(rw) // ROPE_HALF
    ones_bd = jnp.asarray(grp[:, None] == grp[None, :], BF16)
    perm = np.zeros((2 * rw, nh * LANES), np.float32)
    for h in range(nh):
        for j in range(ROPE_HALF):
            perm[h * ROPE_HALF + j, h * LANES + j] = 1.0
            perm[rw + h * ROPE_HALF + j, h * LANES + ROPE_HALF + j] = 1.0
    gkr_pad = jnp.concatenate([g_kr, jnp.zeros((LANES - ROPE_DIM,), g_kr.dtype)]).reshape(1, LANES)
    consts = (gmix.reshape(1, d), wa_pad, g_cq.reshape(1, -1), g_ckv.reshape(1, -1), gkr_pad,
              wuq, g_qn.reshape(1, -1), jnp.tile(g_qr[:ROPE_HALF], nh).reshape(1, rw),
              jnp.tile(g_qr[ROPE_HALF:], nh).reshape(1, rw), wuk, g_kn.reshape(1, -1),
              ones_bd, jnp.asarray(perm, BF16))
    ckv, kr, cbf, qcat, kcat = _mla_proj(x, cos, sin, consts)
    tp = b * l
    hw = nh * HEAD_PAD
    o_p = _attn_prompt(qcat[:tp].reshape(b, l, hw), kcat[:tp].reshape(b, l, hw),
                       cbf[:tp].reshape(b, l, KV_LORA))
    c_s = ckv[tp:].reshape(nb, ds, KV_LORA)
    kr_s = kr[tp:].reshape(nb, ds, ROPE_DIM)
    wukt = w_uk.reshape(KV_LORA, nh * NOPE_DIM).T.astype(BF16)
    o_s = _attn_sample(page_table, qcat[tp:].astype(F32).reshape(nb, ds, hw), c_s, kr_s, wukt,
                       g_kn.reshape(1, -1), cache_ckv, cache_krope, layer=layer)
    o_s = o_s.reshape(nb, nh, ds, KV_LORA).transpose(0, 2, 1, 3).reshape(nb * ds, nh * KV_LORA)
    olat = jnp.concatenate([o_p.reshape(tp, nh * KV_LORA), o_s], axis=0)
    wuv = w_uv.transpose(1, 0, 2).astype(BF16)
    x_new = _mla_out(x, olat, wuv, w_o.astype(BF16))
    return (x_new, ckv[:tp].reshape(b, l, KV_LORA), kr[:tp].reshape(b, l, ROPE_DIM), c_s, kr_s)


def kernel(x_prompt, x_sample, cache_ckv, cache_krope, state_s5_re, state_s5_im, page_table, norm_ffn1, ffn1_w_gate, ffn1_w_up, ffn1_w_down, norm_mix, norm_ffn2, ffn2_w_gate, ffn2_w_up, ffn2_w_down, ab_w_in, gm_norm_g, gm_norm_b, gm_w_s, gm_b_s, s5_lambda_re, s5_lambda_im, s5_log_dt, s5_b_re, s5_b_im, s5_c_re, s5_c_im, s5_d, s5_w_glu, s5_b_glu, ab_w_out, mla_w_a, mla_g_cq, mla_g_ckv, mla_w_uq, mla_g_qn, mla_g_qr, mla_g_kr, mla_w_uk, mla_g_kn, mla_w_uv, mla_w_o):
    b, l, d = x_prompt.shape
    nb, ds, _ = x_sample.shape
    depth = norm_ffn1.shape[0]
    past = page_table.shape[1] * cache_ckv.shape[2]
    x = jnp.concatenate([x_prompt.reshape(b * l, d), x_sample.reshape(nb * ds, d)], axis=0)
    cos, sin = _rope_tables(b, l, nb, ds, past)
    ckv_p, kr_p, ckv_s, kr_s = [], [], [], []
    s5r_p, s5i_p, s5r_s, s5i_s, gmv_s = [], [], [], [], []
    for layer in range(depth):
        x = _ffn(x, norm_ffn1[layer], ffn1_w_gate[layer].astype(BF16), ffn1_w_up[layer].astype(BF16),
                 ffn1_w_down[layer].astype(BF16))
        if layer % 2 == 0:
            i = layer // 2
            x, v_s, hr_p, hi_p, hr_s, hi_s = _even_layer(
                x, b, l, nb, ds, state_s5_re[i], state_s5_im[i], norm_mix[layer], ab_w_in[i],
                gm_norm_g[i], gm_norm_b[i], gm_w_s[i], gm_b_s[i], s5_lambda_re[i], s5_lambda_im[i],
                s5_log_dt[i], s5_b_re[i], s5_b_im[i], s5_c_re[i], s5_c_im[i], s5_d[i], s5_w_glu[i],
                s5_b_glu[i], ab_w_out[i])
            s5r_p.append(hr_p)
            s5i_p.append(hi_p)
            s5r_s.append(hr_s)
            s5i_s.append(hi_s)
            gmv_s.append(v_s)
        else:
            j = layer // 2
            x, c_p, k_p, c_s, k_s = _mla_layer(
                x, b, l, nb, ds, j, cos, sin, norm_mix[layer], cache_ckv, cache_krope, page_table,
                mla_w_a[j], mla_g_cq[j], mla_g_ckv[j], mla_w_uq[j], mla_g_qn[j], mla_g_qr[j],
                mla_g_kr[j], mla_w_uk[j], mla_g_kn[j], mla_w_uv[j], mla_w_o[j])
            ckv_p.append(c_p)
            kr_p.append(k_p)
            ckv_s.append(c_s)
            kr_s.append(k_s)
        x = _ffn(x, norm_ffn2[layer], ffn2_w_gate[layer].astype(BF16), ffn2_w_up[layer].astype(BF16),
                 ffn2_w_down[layer].astype(BF16))
    xp = x[:b * l].reshape(b, l, d)
    xs = x[b * l:].reshape(nb, ds, d)
    return (xp, xs,
            jnp.stack(ckv_p), jnp.stack(kr_p), jnp.stack(ckv_s), jnp.stack(kr_s),
            jnp.stack(s5r_p), jnp.stack(s5i_p), jnp.stack(s5r_s), jnp.stack(s5i_s),
            jnp.stack(gmv_s))
```

```python
import functools
import math

import jax
import jax.numpy as jnp
import numpy as np
from jax import lax
from jax.experimental import pallas as pl
from jax.experimental.pallas import tpu as pltpu

F32 = jnp.float32
BF16 = jnp.bfloat16

EPS = 1e-6
GM_HEADS = 4
CHUNK = 128
S5_CH = 16
S5_STATE = 64
S5_QCHUNKS = 4
MLA_HEADS = 8
Q_LORA = 384
KV_LORA = 256
NOPE_DIM = 128
ROPE_DIM = 64
ROPE_HALF = ROPE_DIM // 2
V_DIM = 128
ROPE_BASE = 10000.0
ATTN_SCALE = (NOPE_DIM + ROPE_DIM) ** -0.5
Q_SCALE = ATTN_SCALE * math.log2(math.e)
HEAD_PAD = 256
LANES = 128
SUBLANES = 8
VMEM_LIMIT = 56 * 1024 * 1024


def _pick(n, prefs):
    for p in prefs:
        if n % p == 0:
            return p
    raise ValueError(f"no tile in {prefs} divides {n}")


def _params(*sem):
    return pltpu.CompilerParams(dimension_semantics=sem, vmem_limit_bytes=VMEM_LIMIT)


def _const_spec(shape):
    n = len(shape)
    return pl.BlockSpec(shape, lambda *_: (0,) * n)


def _rms(x, g):
    return x * lax.rsqrt(jnp.mean(x * x, axis=-1, keepdims=True) + EPS) * g


def _sigmoid(x):
    return 1.0 / (1.0 + jnp.exp(-x))


def _gelu_tanh(x):
    return 0.5 * x * (1.0 + jnp.tanh(math.sqrt(2.0 / math.pi) * (x + 0.044715 * (x * x * x))))


def _dot(a, b):
    return jnp.dot(a, b, preferred_element_type=F32)


def _lane_tile(x, n):
    return jnp.concatenate([x] * n, axis=1)


def _dot_nt(a, b):
    return lax.dot_general(a, b, (((1,), (1,)), ((), ())), preferred_element_type=F32)


def _ffn_body(x_ref, g_ref, wg_ref, wu_ref, wd_ref, o_ref, a_scr, *, fchunk):
    x = x_ref[...]
    h = _rms(x, g_ref[...]).astype(BF16)
    d_ff = wg_ref.shape[1]
    for c in range(0, d_ff, fchunk):
        g = _dot(h, wg_ref[:, c:c + fchunk])
        u = _dot(h, wu_ref[:, c:c + fchunk])
        a_scr[:, c:c + fchunk] = (g * _sigmoid(g) * u).astype(BF16)
    o_ref[...] = x + 0.5 * _dot(a_scr[...], wd_ref[...])


def _ffn(x, g, wg, wu, wd):
    t, d = x.shape
    d_ff = wg.shape[1]
    tm = _pick(t, (512, 256, 128, 64, 8))
    return pl.pallas_call(
        functools.partial(_ffn_body, fchunk=256),
        out_shape=jax.ShapeDtypeStruct((t, d), F32),
        grid=(t // tm,),
        in_specs=[
            pl.BlockSpec((tm, d), lambda i: (i, 0)),
            _const_spec((1, d)),
            _const_spec((d, d_ff)),
            _const_spec((d, d_ff)),
            _const_spec((d_ff, d)),
        ],
        out_specs=pl.BlockSpec((tm, d), lambda i: (i, 0)),
        scratch_shapes=[pltpu.VMEM((tm, d_ff), BF16)],
        compiler_params=_params("arbitrary"),
        name="ffn",
    )(x, g.reshape(1, d), wg, wu, wd)


def _s5_tables(lam_re, lam_im, log_dt, b_re, b_im, c_re, c_im):
    lr, li = lam_re.astype(F32), lam_im.astype(F32)
    dt = jnp.exp(log_dt.astype(F32))[:, None]
    mag = jnp.exp(lr * dt)
    ar, ai = mag * jnp.cos(li * dt), mag * jnp.sin(li * dt)
    er, ei = ar - 1.0, ai
    den = lr * lr + li * li
    qr = (er * lr + ei * li) / den
    qi = (ei * lr - er * li) / den
    br, bi = b_re.astype(F32), b_im.astype(F32)
    bbr = qr[..., None] * br - qi[..., None] * bi
    bbi = qr[..., None] * bi + qi[..., None] * br
    g = lr.shape[0]
    gq = g // S5_QCHUNKS
    eye = jnp.eye(gq, dtype=F32)

    def b_blk(bb):
        bq = bb.reshape(S5_QCHUNKS, gq, S5_STATE, S5_CH)
        return jnp.einsum('qgnc,gh->qhcgn', bq, eye).reshape(S5_QCHUNKS, gq * S5_CH, gq * S5_STATE)

    def c_blk(cc):
        cq = cc.astype(F32).reshape(S5_QCHUNKS, gq, S5_CH, S5_STATE)
        return jnp.einsum('qgcn,gh->qgnhc', cq, eye).reshape(S5_QCHUNKS, gq * S5_STATE, gq * S5_CH)

    bblk = jnp.concatenate([b_blk(bbr), b_blk(bbi)], axis=2).astype(BF16)
    cblk = jnp.concatenate([c_blk(c_re), -c_blk(c_im)], axis=1).astype(BF16)
    a_tab = jnp.stack([ar.reshape(S5_QCHUNKS, -1), ai.reshape(S5_QCHUNKS, -1)], axis=1)
    return a_tab, bblk, cblk


def _pack_state(hr, hi):
    n = hr.shape[0]
    r = hr.astype(F32).reshape(n, S5_QCHUNKS, 1, -1)
    i = hi.astype(F32).reshape(n, S5_QCHUNKS, 1, -1)
    return jnp.concatenate([r, i], axis=2).reshape(n, -1)


def _unpack_state(st, groups):
    n = st.shape[0]
    s4 = st.reshape(n, S5_QCHUNKS, 2, -1)
    return (s4[:, :, 0].reshape(n, groups, S5_STATE), s4[:, :, 1].reshape(n, groups, S5_STATE))


def _cmul(ar, ai, br, bi):
    return ar * br - ai * bi, ar * bi + ai * br


def _even_prompt_body(x_ref, gmix_ref, win_ref, lng_ref, lnb_ref, wtril_ref, sbias_ref,
                      atab_ref, bblk_ref, cblk_ref, d_ref, wglu_ref, bglu_ref, wout_ref,
                      o_ref, st_ref,
                      us_scr, usp_scr, xs_scr, yp_scr, y_scr, cat_scr, carry_scr, *, tt):
    m = tt // SUBLANES
    gw = lng_ref.shape[1]
    sw = d_ref.shape[1]
    hs = sw
    t_idx = pl.program_id(1)

    @pl.when(t_idx == 0)
    def _():
        carry_scr[...] = jnp.zeros_like(carry_scr)

    x = x_ref[...]
    hn = _rms(x, gmix_ref[...]).astype(BF16)
    z = _dot(hn, win_ref[...])
    u = _gelu_tanh(z[:, :gw])
    v = _gelu_tanh(z[:, gw:2 * gw])
    mu = jnp.mean(v, axis=-1, keepdims=True)
    vc = v - mu
    v = vc * lax.rsqrt(jnp.mean(vc * vc, axis=-1, keepdims=True) + EPS) * lng_ref[...] + lnb_ref[...]
    vb = v.astype(BF16)
    hd = gw // GM_HEADS
    for c in range(tt // CHUNK):
        rows = slice(c * CHUNK, (c + 1) * CHUNK)
        for h in range(GM_HEADS):
            cols = slice(h * hd, (h + 1) * hd)
            s = _dot(wtril_ref[h], vb[rows, cols]) + sbias_ref[:, cols]
            cat_scr[rows, cols] = (u[rows, cols] * s).astype(BF16)

    qw = sw // S5_QCHUNKS
    for q in range(S5_QCHUNKS):
        us_scr[q] = z[:, 2 * gw + q * qw:2 * gw + (q + 1) * qw]
    sub = lax.broadcasted_iota(jnp.int32, (SUBLANES, hs), 0)

    def shift_rows(val, k):
        return jnp.where(sub >= k, pltpu.roll(val, k, axis=0), 0.0)

    for q in range(S5_QCHUNKS):
        for i in range(m):
            usp_scr[i * SUBLANES:(i + 1) * SUBLANES, :] = us_scr.at[q][pl.ds(i, SUBLANES, stride=m), :]
        xs_scr[...] = _dot(usp_scr[...].astype(BF16), bblk_ref[q])
        ar = atab_ref[q, 0:1, :]
        ai = atab_ref[q, 1:2, :]
        arb = jnp.broadcast_to(ar, (SUBLANES, hs))
        aib = jnp.broadcast_to(ai, (SUBLANES, hs))

        def step(i, carry, store):
            hr, hi = carry
            r0 = pl.multiple_of(i * SUBLANES, SUBLANES)
            xr = xs_scr[pl.ds(r0, SUBLANES), 0:hs]
            xi = xs_scr[pl.ds(r0, SUBLANES), hs:2 * hs]
            nr = arb * hr - aib * hi + xr
            ni = arb * hi + aib * hr + xi
            if store:
                xs_scr[pl.ds(r0, SUBLANES), 0:hs] = nr
                xs_scr[pl.ds(r0, SUBLANES), hs:2 * hs] = ni
            return nr, ni

        zero = jnp.zeros((SUBLANES, hs), F32)
        er, ei = lax.fori_loop(0, m, functools.partial(step, store=False), (zero, zero), unroll=True)

        pr, pi = ar, ai
        for _ in range(int(math.log2(m))):
            pr, pi = _cmul(pr, pi, pr, pi)
        a1 = (pr, pi)
        a2 = _cmul(*a1, *a1)
        a4 = _cmul(*a2, *a2)
        fr, fi = er, ei
        for k, (kr, ki) in ((1, a1), (2, a2), (4, a4)):
            sr, si = shift_rows(fr, k), shift_rows(fi, k)
            dr, di = _cmul(kr, ki, sr, si)
            fr, fi = fr + dr, fi + di
        wr, wi = jnp.ones((SUBLANES, hs), F32), jnp.zeros((SUBLANES, hs), F32)
        for bit, (kr, ki) in ((1, a1), (2, a2), (4, a4)):
            sel = (sub & bit) != 0
            wr, wi = _cmul(wr, wi, jnp.where(sel, kr, 1.0), jnp.where(sel, ki, 0.0))
        cr = carry_scr[q, SUBLANES - 1:SUBLANES, 0:hs]
        ci = carry_scr[q, SUBLANES - 1:SUBLANES, hs:2 * hs]
        tr, ti = _cmul(wr, wi, cr, ci)
        h0r = tr + shift_rows(fr, 1)
        h0i = ti + shift_rows(fi, 1)
        nr, ni = _cmul(a1[0], a1[1], h0r, h0i)
        carry_scr[q, :, 0:hs] = nr + er
        carry_scr[q, :, hs:2 * hs] = ni + ei
        st_ref[0, :, q * 2 * hs:(q + 1) * 2 * hs] = carry_scr[q, SUBLANES - 1:SUBLANES, :]

        lax.fori_loop(0, m, functools.partial(step, store=True), (h0r, h0i), unroll=True)
        yp_scr[...] = _dot(xs_scr[...].astype(BF16), cblk_ref[q])
        for s in range(SUBLANES):
            y_scr[s * m:(s + 1) * m, q * qw:(q + 1) * qw] = (
                yp_scr[pl.ds(s, m, stride=SUBLANES), :]
                + d_ref[:, q * qw:(q + 1) * qw] * us_scr[q, s * m:(s + 1) * m, :])

    g = _gelu_tanh(y_scr[...])
    gate = _sigmoid(_dot(g.astype(BF16), wglu_ref[...]) + bglu_ref[...])
    cat_scr[:, gw:gw + sw] = (g * gate).astype(BF16)
    o_ref[...] = x + _dot(cat_scr[...], wout_ref[...])


def _even_prompt(x, b, l, gmix, win, lng, lnb, wtril, sbias, atab, bblk, cblk, dvec, wglu, bglu, wout):
    d = x.shape[1]
    gw, sw = lng.shape[1], dvec.shape[1]
    tt = _pick(l, (256, 128))
    nt = l // tt
    hs2 = 2 * sw
    args = (x, gmix, win, lng, lnb, wtril, sbias, atab, bblk, cblk, dvec, wglu, bglu, wout)
    in_specs = [pl.BlockSpec((tt, d), lambda bi, ti: (bi * nt + ti, 0))]
    in_specs += [_const_spec(a.shape) for a in args[1:]]
    return pl.pallas_call(
        functools.partial(_even_prompt_body, tt=tt),
        out_shape=(jax.ShapeDtypeStruct((b * l, d), F32),
                   jax.ShapeDtypeStruct((b, 1, S5_QCHUNKS * hs2), F32)),
        grid=(b, nt),
        in_specs=in_specs,
        out_specs=(pl.BlockSpec((tt, d), lambda bi, ti: (bi * nt + ti, 0)),
                   pl.BlockSpec((1, 1, S5_QCHUNKS * hs2), lambda bi, ti: (bi, 0, 0))),
        scratch_shapes=[
            pltpu.VMEM((S5_QCHUNKS, tt, sw // S5_QCHUNKS), F32),
            pltpu.VMEM((tt, sw // S5_QCHUNKS), F32),
            pltpu.VMEM((tt, hs2), F32),
            pltpu.VMEM((tt, sw // S5_QCHUNKS), F32),
            pltpu.VMEM((tt, sw), F32),
            pltpu.VMEM((tt, gw + sw), BF16),
            pltpu.VMEM((S5_QCHUNKS, SUBLANES, hs2), F32),
        ],
        compiler_params=_params("arbitrary", "arbitrary"),
        name="even_prompt",
    )(*args)


def _even_sample_body(x_ref, h0_ref, gmix_ref, win_ref, lng_ref, lnb_ref, scoef_ref, sbias_ref,
                      atab_ref, bblk_ref, cblk_ref, d_ref, wglu_ref, bglu_ref, wout_ref,
                      o_ref, v_ref, st_ref,
                      xs_scr, y_scr, cat_scr, *, ds, nb):
    gw = lng_ref.shape[1]
    sw = d_ref.shape[1]
    hs = sw
    x = x_ref[...]
    hn = _rms(x, gmix_ref[...]).astype(BF16)
    z = _dot(hn, win_ref[...])
    u = _gelu_tanh(z[:, :gw])
    v = _gelu_tanh(z[:, gw:2 * gw])
    mu = jnp.mean(v, axis=-1, keepdims=True)
    vc = v - mu
    v = vc * lax.rsqrt(jnp.mean(vc * vc, axis=-1, keepdims=True) + EPS) * lng_ref[...] + lnb_ref[...]
    v_ref[...] = v
    for t in range(ds):
        s = jnp.broadcast_to(sbias_ref[t:t + 1, :], (nb, gw))
        for sp in range(t + 1):
            s = s + scoef_ref[t * ds + sp:t * ds + sp + 1, :] * v[sp * nb:(sp + 1) * nb, :]
        cat_scr[t * nb:(t + 1) * nb, 0:gw] = (u[t * nb:(t + 1) * nb, :] * s).astype(BF16)

    us = z[:, 2 * gw:]
    usb = us.astype(BF16)
    qw = sw // S5_QCHUNKS
    for q in range(S5_QCHUNKS):
        xs_scr[...] = _dot(usb[:, q * qw:(q + 1) * qw], bblk_ref[q])
        ar = atab_ref[q, 0:1, :]
        ai = atab_ref[q, 1:2, :]
        hr = h0_ref[:, q * 2 * hs:q * 2 * hs + hs]
        hi = h0_ref[:, q * 2 * hs + hs:(q + 1) * 2 * hs]
        for t in range(ds):
            rows = slice(t * nb, (t + 1) * nb)
            nr = ar * hr - ai * hi + xs_scr[rows, 0:hs]
            ni = ar * hi + ai * hr + xs_scr[rows, hs:2 * hs]
            xs_scr[rows, 0:hs] = nr
            xs_scr[rows, hs:2 * hs] = ni
            hr, hi = nr, ni
        st_ref[:, q * 2 * hs:q * 2 * hs + hs] = hr
        st_ref[:, q * 2 * hs + hs:(q + 1) * 2 * hs] = hi
        y_scr[:, q * qw:(q + 1) * qw] = _dot(xs_scr[...].astype(BF16), cblk_ref[q])
    g = _gelu_tanh(y_scr[...] + d_ref[...] * us)
    gate = _sigmoid(_dot(g.astype(BF16), wglu_ref[...]) + bglu_ref[...])
    cat_scr[:, gw:gw + sw] = (g * gate).astype(BF16)
    o_ref[...] = x + _dot(cat_scr[...], wout_ref[...])


def _even_sample(xs_tm, h0, gmix, win, lng, lnb, scoef, sbias8, atab, bblk, cblk, dvec, wglu, bglu, wout, *, ds):
    t, d = xs_tm.shape
    nb = t // ds
    gw, sw = lng.shape[1], dvec.shape[1]
    args = (xs_tm, h0, gmix, win, lng, lnb, scoef, sbias8, atab, bblk, cblk, dvec, wglu, bglu, wout)
    return pl.pallas_call(
        functools.partial(_even_sample_body, ds=ds, nb=nb),
        out_shape=(jax.ShapeDtypeStruct((t, d), F32),
                   jax.ShapeDtypeStruct((t, gw), F32),
                   jax.ShapeDtypeStruct(h0.shape, F32)),
        grid=(1,),
        in_specs=[_const_spec(a.shape) for a in args],
        out_specs=(_const_spec((t, d)), _const_spec((t, gw)), _const_spec(h0.shape)),
        scratch_shapes=[
            pltpu.VMEM((t, 2 * sw), F32),
            pltpu.VMEM((t, sw), F32),
            pltpu.VMEM((t, gw + sw), BF16),
        ],
        compiler_params=_params("arbitrary"),
        name="even_sample",
    )(*args)


def _mla_proj_body(x_ref, cos_ref, sin_ref, gmix_ref, wa_ref, gcq_ref, gckv_ref, gkr_ref,
                   wuq_ref, gqn_ref, gq1_ref, gq2_ref, wuk_ref, gkn_ref, ones_ref, perm_ref,
                   ckv_ref, kr_ref, cbf_ref, qcat_ref, kcat_ref):
    tm = x_ref.shape[0]
    x = x_ref[...]
    hn = _rms(x, gmix_ref[...]).astype(BF16)
    a = _dot(hn, wa_ref[...])
    cq = _rms(a[:, :Q_LORA], gcq_ref[...])
    ckv = _rms(a[:, Q_LORA:Q_LORA + KV_LORA], gckv_ref[...])
    ckv_ref[...] = ckv
    ckv_b = ckv.astype(BF16)
    cbf_ref[...] = ckv_b

    cos = cos_ref[...]
    sin = sin_ref[...]
    slab = a[:, Q_LORA + KV_LORA:]
    ms = jnp.sum(slab * slab, axis=-1, keepdims=True) * (1.0 / ROPE_DIM)
    kn = slab * lax.rsqrt(ms + EPS) * gkr_ref[...]
    lane = lax.broadcasted_iota(jnp.int32, (tm, LANES), 1)
    rot = jnp.where(lane < ROPE_HALF, -pltpu.roll(kn, LANES - ROPE_HALF, axis=1),
                    pltpu.roll(kn, ROPE_HALF, axis=1))
    kr = kn * cos + rot * sin
    kr_ref[...] = kr[:, :ROPE_DIM]
    kr_pad = jnp.where(lane < ROPE_DIM, kr, 0.0).astype(BF16)

    q = _dot(cq.astype(BF16), wuq_ref[...])
    k = _dot(ckv_b, wuk_ref[...])
    nn = MLA_HEADS * NOPE_DIM
    rw = MLA_HEADS * ROPE_HALF
    x1 = q[:, nn:nn + rw]
    x2 = q[:, nn + rw:nn + 2 * rw]
    ssq = x1 * x1 + x2 * x2
    ssq_hi = ssq.astype(BF16)
    ssq_lo = (ssq - ssq_hi.astype(F32)).astype(BF16)
    gsum = _dot(ssq_hi, ones_ref[...]) + _dot(ssq_lo, ones_ref[...])
    r = lax.rsqrt(gsum * (1.0 / ROPE_DIM) + EPS)
    x1n = x1 * r * gq1_ref[...]
    x2n = x2 * r * gq2_ref[...]
    reps = rw // LANES
    cos2 = jnp.concatenate([cos] * reps, axis=1)
    sin2 = jnp.concatenate([sin] * reps, axis=1)
    o1 = (x1n * cos2 - x2n * sin2) * Q_SCALE
    o2 = (x2n * cos2 + x1n * sin2) * Q_SCALE
    qr_cat = _dot(jnp.concatenate([o1, o2], axis=1).astype(BF16), perm_ref[...]).astype(BF16)
    for h in range(MLA_HEADS):
        nsl = slice(h * NOPE_DIM, (h + 1) * NOPE_DIM)
        qcat_ref[:, h * HEAD_PAD:h * HEAD_PAD + NOPE_DIM] = (_rms(q[:, nsl], gqn_ref[...]) * Q_SCALE).astype(BF16)
        qcat_ref[:, h * HEAD_PAD + NOPE_DIM:(h + 1) * HEAD_PAD] = qr_cat[:, h * LANES:(h + 1) * LANES]
        kcat_ref[:, h * HEAD_PAD:h * HEAD_PAD + NOPE_DIM] = _rms(k[:, nsl], gkn_ref[...]).astype(BF16)
        kcat_ref[:, h * HEAD_PAD + NOPE_DIM:(h + 1) * HEAD_PAD] = kr_pad


def _mla_proj(x, cos, sin, consts):
    t, d = x.shape
    tm = _pick(t, (256, 128, 64, 8))
    hw = MLA_HEADS * HEAD_PAD
    row = lambda w: pl.BlockSpec((tm, w), lambda i: (i, 0))
    return pl.pallas_call(
        _mla_proj_body,
        out_shape=(jax.ShapeDtypeStruct((t, KV_LORA), F32),
                   jax.ShapeDtypeStruct((t, ROPE_DIM), F32),
                   jax.ShapeDtypeStruct((t, KV_LORA), BF16),
                   jax.ShapeDtypeStruct((t, hw), BF16),
                   jax.ShapeDtypeStruct((t, hw), BF16)),
        grid=(t // tm,),
        in_specs=[row(d), row(LANES), row(LANES)] + [_const_spec(c.shape) for c in consts],
        out_specs=(row(KV_LORA), row(ROPE_DIM), row(KV_LORA), row(hw), row(hw)),
        compiler_params=_params("arbitrary"),
        name="mla_proj",
    )(x, cos, sin, *consts)


def _attn_prompt_body(qi_ref, ki_ref, q_ref, k_ref, c_ref, o_ref, m_scr, l_scr, acc_scr, *, tq):
    p = pl.program_id(1)
    qi = qi_ref[p]
    ki = ki_ref[p]

    @pl.when(ki == 0)
    def _():
        m_scr[...] = jnp.full_like(m_scr, -jnp.inf)
        l_scr[...] = jnp.zeros_like(l_scr)
        acc_scr[...] = jnp.zeros_like(acc_scr)

    def tile(masked):
        c = c_ref[...]
        nrep = tq // LANES
        vrep = KV_LORA // LANES
        if masked:
            row = lax.broadcasted_iota(jnp.int32, (tq, tq), 0)
            col = lax.broadcasted_iota(jnp.int32, (tq, tq), 1)
            keep = row >= col
        for h in range(MLA_HEADS):
            hsl = slice(h * HEAD_PAD, (h + 1) * HEAD_PAD)
            s = _dot_nt(q_ref[:, hsl], k_ref[:, hsl])
            if masked:
                s = jnp.where(keep, s, -jnp.inf)
            m_prev = m_scr[h]
            m_new = jnp.maximum(m_prev, jnp.max(s, axis=-1, keepdims=True))
            alpha = jnp.exp2(m_prev - m_new)
            pexp = jnp.exp2(s - _lane_tile(m_new, nrep))
            psum = pexp[:, 0:LANES]
            for j in range(1, nrep):
                psum = psum + pexp[:, j * LANES:(j + 1) * LANES]
            l_scr[h] = alpha * l_scr[h] + psum
            acc_scr[h] = _lane_tile(alpha, vrep) * acc_scr[h] + _dot(pexp.astype(BF16), c)
            m_scr[h] = m_new

    @pl.when(ki < qi)
    def _():
        tile(False)

    @pl.when(ki == qi)
    def _():
        tile(True)
        for h in range(MLA_HEADS):
            o_ref[:, h * KV_LORA:(h + 1) * KV_LORA] = (
                acc_scr[h] / jnp.sum(l_scr[h], axis=-1, keepdims=True)).astype(BF16)


def _attn_prompt(qcat, kcat, cbf, b, l):
    hw = qcat.shape[1]
    tq = _pick(l, (512, 256, 128))
    nq = l // tq
    pairs = [(i, j) for i in range(nq) for j in range(i + 1)]
    qi_tab = jnp.asarray([i for i, _ in pairs], jnp.int32)
    ki_tab = jnp.asarray([j for _, j in pairs], jnp.int32)
    grid_spec = pltpu.PrefetchScalarGridSpec(
        num_scalar_prefetch=2,
        grid=(b, len(pairs)),
        in_specs=[
            pl.BlockSpec((tq, hw), lambda bi, p, qi, ki: (bi * nq + qi[p], 0)),
            pl.BlockSpec((tq, hw), lambda bi, p, qi, ki: (bi * nq + ki[p], 0)),
            pl.BlockSpec((tq, KV_LORA), lambda bi, p, qi, ki: (bi * nq + ki[p], 0)),
        ],
        out_specs=pl.BlockSpec((tq, MLA_HEADS * KV_LORA), lambda bi, p, qi, ki: (bi * nq + qi[p], 0)),
        scratch_shapes=[
            pltpu.VMEM((MLA_HEADS, tq, LANES), F32),
            pltpu.VMEM((MLA_HEADS, tq, LANES), F32),
            pltpu.VMEM((MLA_HEADS, tq, KV_LORA), F32),
        ],
    )
    return pl.pallas_call(
        functools.partial(_attn_prompt_body, tq=tq),
        out_shape=jax.ShapeDtypeStruct((b * l, MLA_HEADS * KV_LORA), BF16),
        grid_spec=grid_spec,
        compiler_params=_params("arbitrary", "arbitrary"),
        name="attn_prompt",
    )(qi_tab, ki_tab, qcat, kcat, cbf)


def _attn_sample_body(pt_ref, q_ref, cn_ref, krn_ref, wukt_ref, gkn_ref, ckv_hbm, krc_hbm,
                      o_ref, cbuf, krbuf, sems, lhs_scr, big_scr, tail_c, tail_kr,
                      *, layer, n_pages, page, tk, ds):
    b = pl.program_id(0)
    nb = pl.num_programs(0)
    slot = lax.rem(b, 2)
    nh = MLA_HEADS
    nrow = nh * ds
    kw = nh * NOPE_DIM
    ppt = tk // page
    nt = n_pages // ppt

    def page_copies(seq, sl, j):
        pg = pt_ref[seq, j]
        r0 = pl.multiple_of(j * page, page)
        return (pltpu.make_async_copy(ckv_hbm.at[layer, pg], cbuf.at[sl, pl.ds(r0, page), :], sems.at[0, sl]),
                pltpu.make_async_copy(krc_hbm.at[layer, pg], krbuf.at[sl, j], sems.at[1, sl]))

    def start_fetch(seq, sl):
        def body(j, carry):
            for cp in page_copies(seq, sl, j):
                cp.start()
            return carry
        lax.fori_loop(0, n_pages, body, 0)

    @pl.when(b == 0)
    def _():
        start_fetch(0, 0)
        lhs_scr[0:kw, :] = wukt_ref[...]
        tail_c[...] = jnp.zeros_like(tail_c)
        tail_kr[...] = jnp.zeros_like(tail_kr)

    for j in range(n_pages):
        for cp in page_copies(b, slot, j):
            cp.wait()

    @pl.when(b + 1 < nb)
    def _():
        start_fetch(b + 1, 1 - slot)

    qf = q_ref[0]
    qabs, qrope = [], []
    for h in range(nh):
        qn = (qf[:, h * HEAD_PAD:h * HEAD_PAD + NOPE_DIM] * gkn_ref[...]).astype(BF16)
        qabs.append(_dot(qn, wukt_ref[h * NOPE_DIM:(h + 1) * NOPE_DIM, :]))
        qrope.append(qf[:, h * HEAD_PAD + NOPE_DIM:h * HEAD_PAD + NOPE_DIM + ROPE_DIM])
    lhs_scr[kw:kw + nrow, :] = jnp.concatenate(qabs, axis=0).astype(BF16)
    qr_b = jnp.concatenate(qrope, axis=0).astype(BF16)
    tail_c[0:ds, :] = cn_ref[0]
    tail_kr[:, 0:ds] = krn_ref[0]

    def softmax_update(k_rows, sraw, cb, krt, carry, width, masked):
        m_prev, l_prev, acc = carry
        rinv = []
        for h in range(nh):
            kt = k_rows(h)
            ss = jnp.sum(kt * kt, axis=0, keepdims=True) * (1.0 / NOPE_DIM)
            rinv.append(jnp.broadcast_to(lax.rsqrt(ss + EPS), (ds, width)))
        rinv = jnp.concatenate(rinv, axis=0)
        s = sraw * rinv + _dot(qr_b, krt.astype(BF16))
        if masked:
            qpos = lax.rem(lax.broadcasted_iota(jnp.int32, (nrow, width), 0), ds)
            kpos = lax.broadcasted_iota(jnp.int32, (nrow, width), 1)
            s = jnp.where(kpos <= qpos, s, -jnp.inf)
        m_new = jnp.maximum(m_prev, jnp.max(s, axis=-1, keepdims=True))
        alpha = jnp.exp2(m_prev - m_new)
        pexp = jnp.exp2(s - m_new)
        l_new = alpha * l_prev + jnp.sum(pexp, axis=-1, keepdims=True)
        acc = alpha * acc + _dot(pexp.astype(BF16), cb)
        return m_new, l_new, acc

    def c_tile(t):
        return cbuf[slot, t * tk:(t + 1) * tk, :].astype(BF16)

    def scores_into(t, sb):
        big_scr[sb] = _dot_nt(lhs_scr[...], c_tile(t))

    def consume(t, sb, carry):
        krt = jnp.concatenate([krbuf[slot, t * ppt + d] for d in range(ppt)], axis=1)
        return softmax_update(lambda h: big_scr[sb, h * NOPE_DIM:(h + 1) * NOPE_DIM, :],
                              big_scr[sb, kw:kw + nrow, :], c_tile(t), krt, carry, tk, False)

    carry = (jnp.full((nrow, 1), -jnp.inf, F32), jnp.zeros((nrow, 1), F32),
             jnp.zeros((nrow, KV_LORA), F32))
    scores_into(0, 0)
    for t in range(nt):
        if t + 1 < nt:
            scores_into(t + 1, (t + 1) % 2)
        carry = consume(t, t % 2, carry)
    tcb = tail_c[...].astype(BF16)
    tbig = _dot_nt(lhs_scr[...], tcb)
    _, l_fin, acc = softmax_update(lambda h: tbig[h * NOPE_DIM:(h + 1) * NOPE_DIM, :],
                                   tbig[kw:kw + nrow, :], tcb, tail_kr[...], carry,
                                   tail_c.shape[0], True)
    out = acc / l_fin
    for h in range(nh):
        o_ref[:, h * KV_LORA:(h + 1) * KV_LORA] = out[h * ds:(h + 1) * ds, :]


def _attn_sample(page_table, q_s, c_new, krt_new, wukt, gkn, cache_ckv, cache_krope_t, *, layer):
    nb, ds, hw = q_s.shape
    n_pages = page_table.shape[1]
    page = cache_ckv.shape[2]
    tk = _pick(n_pages * page, (512, 256, 128))
    nrow = MLA_HEADS * ds
    kw = MLA_HEADS * NOPE_DIM
    grid_spec = pltpu.PrefetchScalarGridSpec(
        num_scalar_prefetch=1,
        grid=(nb,),
        in_specs=[
            pl.BlockSpec((1, ds, hw), lambda i, pt: (i, 0, 0)),
            pl.BlockSpec((1, ds, KV_LORA), lambda i, pt: (i, 0, 0)),
            pl.BlockSpec((1, ROPE_DIM, ds), lambda i, pt: (i, 0, 0)),
            pl.BlockSpec((kw, KV_LORA), lambda i, pt: (0, 0)),
            pl.BlockSpec((1, NOPE_DIM), lambda i, pt: (0, 0)),
            pl.BlockSpec(memory_space=pl.ANY),
            pl.BlockSpec(memory_space=pl.ANY),
        ],
        out_specs=pl.BlockSpec((ds, MLA_HEADS * KV_LORA), lambda i, pt: (i, 0)),
        scratch_shapes=[
            pltpu.VMEM((2, n_pages * page, KV_LORA), F32),
            pltpu.VMEM((2, n_pages, ROPE_DIM, page), F32),
            pltpu.SemaphoreType.DMA((2, 2)),
            pltpu.VMEM((kw + nrow, KV_LORA), BF16),
            pltpu.VMEM((2, kw + nrow, tk), F32),
            pltpu.VMEM((LANES, KV_LORA), F32),
            pltpu.VMEM((ROPE_DIM, LANES), F32),
        ],
    )
    return pl.pallas_call(
        functools.partial(_attn_sample_body, layer=layer, n_pages=n_pages, page=page, tk=tk, ds=ds),
        out_shape=jax.ShapeDtypeStruct((nb * ds, MLA_HEADS * KV_LORA), F32),
        grid_spec=grid_spec,
        compiler_params=_params("arbitrary"),
        name="attn_sample",
    )(page_table, q_s, c_new, krt_new, wukt, gkn, cache_ckv, cache_krope_t)


def _mla_out_body(x_ref, op_ref, os_ref, wuv_ref, wo_ref, out_ref, v_scr, *, n_prompt_tiles):
    i = pl.program_id(0)

    def up_project(o_ref):
        for h in range(MLA_HEADS):
            v_scr[:, h * V_DIM:(h + 1) * V_DIM] = _dot(
                o_ref[:, h * KV_LORA:(h + 1) * KV_LORA].astype(BF16), wuv_ref[h]).astype(BF16)

    @pl.when(i < n_prompt_tiles)
    def _():
        up_project(op_ref)

    @pl.when(i >= n_prompt_tiles)
    def _():
        up_project(os_ref)

    out_ref[...] = x_ref[...] + _dot(v_scr[...], wo_ref[...])


def _mla_out(x, olat_p, olat_s, wuv, wo):
    t, d = x.shape
    tp, ow = olat_p.shape
    ts = olat_s.shape[0]
    tm = _pick(math.gcd(tp, ts), (512, 256, 128, 64, 32, 16, 8))
    n_p = tp // tm
    return pl.pallas_call(
        functools.partial(_mla_out_body, n_prompt_tiles=n_p),
        out_shape=jax.ShapeDtypeStruct((t, d), F32),
        grid=(t // tm,),
        in_specs=[pl.BlockSpec((tm, d), lambda i: (i, 0)),
                  pl.BlockSpec((tm, ow), lambda i: (jnp.minimum(i, n_p - 1), 0)),
                  pl.BlockSpec((tm, ow), lambda i: (jnp.maximum(i - n_p, 0), 0)),
                  _const_spec(wuv.shape), _const_spec(wo.shape)],
        out_specs=pl.BlockSpec((tm, d), lambda i: (i, 0)),
        scratch_shapes=[pltpu.VMEM((tm, MLA_HEADS * V_DIM), BF16)],
        compiler_params=_params("arbitrary"),
        name="mla_out",
    )(x, olat_p, olat_s, wuv, wo)


def _even_layer(x, b, l, nb, ds, h0r, h0i, gmix, w_in, gm_g, gm_b, gm_w_s, gm_b_s,
                lam_re, lam_im, log_dt, b_re, b_im, c_re, c_im, dvec, w_glu, b_glu, w_out):
    d = x.shape[1]
    gw = gm_g.shape[0]
    sw = dvec.shape[0]
    groups = lam_re.shape[0]
    atab, bblk, cblk = _s5_tables(lam_re, lam_im, log_dt, b_re, b_im, c_re, c_im)
    causal = jnp.tril(jnp.ones((CHUNK, CHUNK), dtype=bool))
    w_tril = jnp.where(causal[None], gm_w_s, 0.0)
    sbias = jnp.repeat(gm_b_s.T, gw // GM_HEADS, axis=1)
    scoef = jnp.repeat(w_tril[:, :ds, :ds].transpose(1, 2, 0).reshape(ds * ds, GM_HEADS),
                       gw // GM_HEADS, axis=1)
    common = (gmix.reshape(1, d), w_in.astype(BF16), gm_g.reshape(1, gw), gm_b.reshape(1, gw))
    tail = (atab, bblk, cblk, dvec.reshape(1, sw), w_glu.astype(BF16), b_glu.reshape(1, sw),
            w_out.astype(BF16))
    yp, st_p = _even_prompt(x, b, l, *common, w_tril.astype(BF16), sbias, *tail)
    xs_tm = x[b * l:].reshape(nb, ds, d).transpose(1, 0, 2).reshape(ds * nb, d)
    ys_tm, v_tm, st_s = _even_sample(xs_tm, _pack_state(h0r, h0i), *common, scoef, sbias[:ds], *tail, ds=ds)
    ys = ys_tm.reshape(ds, nb, d).transpose(1, 0, 2).reshape(nb * ds, d)
    v_s = v_tm.reshape(ds, nb, gw).transpose(1, 0, 2)
    x_new = jnp.concatenate([yp, ys], axis=0)
    hr_p, hi_p = _unpack_state(st_p.reshape(b, -1), groups)
    hr_s, hi_s = _unpack_state(st_s, groups)
    return x_new, v_s, hr_p, hi_p, hr_s, hi_s


def _rope_tables(b, l, nb, ds, past):
    inv = ROPE_BASE ** (-jnp.arange(ROPE_HALF, dtype=F32) * (2.0 / ROPE_DIM))
    pos_p = jnp.arange(l, dtype=jnp.int32)
    pos_s = past + jnp.arange(ds, dtype=jnp.int32)
    pos = jnp.concatenate([jnp.tile(pos_p, b), jnp.tile(pos_s, nb)])
    ang = pos.astype(F32)[:, None] * inv[None, :]
    reps = LANES // ROPE_HALF
    return jnp.tile(jnp.cos(ang), (1, reps)), jnp.tile(jnp.sin(ang), (1, reps))


def _mla_layer(x, b, l, nb, ds, layer, cos, sin, gmix, cache_ckv, cache_krope_t, page_table,
               w_a, g_cq, g_ckv, w_uq, g_qn, g_qr, g_kr, w_uk, g_kn, w_uv, w_o):
    d = x.shape[1]
    nh = MLA_HEADS
    wa_pad = jnp.concatenate([w_a, jnp.zeros((d, LANES - ROPE_DIM), w_a.dtype)], axis=1).astype(BF16)
    wq3 = w_uq.reshape(Q_LORA, nh, NOPE_DIM + ROPE_DIM)
    wuq = jnp.concatenate([wq3[:, :, :NOPE_DIM].reshape(Q_LORA, -1),
                           wq3[:, :, NOPE_DIM:NOPE_DIM + ROPE_HALF].reshape(Q_LORA, -1),
                           wq3[:, :, NOPE_DIM + ROPE_HALF:].reshape(Q_LORA, -1)], axis=1).astype(BF16)
    wuk = w_uk.reshape(KV_LORA, nh * NOPE_DIM).astype(BF16)
    rw = nh * ROPE_HALF
    grp = np.arange(rw) // ROPE_HALF
    ones_bd = jnp.asarray(grp[:, None] == grp[None, :], BF16)
    perm = np.zeros((2 * rw, nh * LANES), np.float32)
    for h in range(nh):
        for j in range(ROPE_HALF):
            perm[h * ROPE_HALF + j, h * LANES + j] = 1.0
            perm[rw + h * ROPE_HALF + j, h * LANES + ROPE_HALF + j] = 1.0
    gkr_pad = jnp.concatenate([g_kr, jnp.zeros((LANES - ROPE_DIM,), g_kr.dtype)]).reshape(1, LANES)
    consts = (gmix.reshape(1, d), wa_pad, g_cq.reshape(1, -1), g_ckv.reshape(1, -1), gkr_pad,
              wuq, g_qn.reshape(1, -1), jnp.tile(g_qr[:ROPE_HALF], nh).reshape(1, rw),
              jnp.tile(g_qr[ROPE_HALF:], nh).reshape(1, rw), wuk, g_kn.reshape(1, -1),
              ones_bd, jnp.asarray(perm, BF16))
    ckv, kr, cbf, qcat, kcat = _mla_proj(x, cos, sin, consts)
    tp = b * l
    hw = nh * HEAD_PAD
    o_p = _attn_prompt(qcat, kcat, cbf, b, l)
    c_s = ckv[tp:].reshape(nb, ds, KV_LORA)
    kr_s = kr[tp:].reshape(nb, ds, ROPE_DIM)
    wukt = w_uk.reshape(KV_LORA, nh * NOPE_DIM).T.astype(BF16)
    o_s = _attn_sample(page_table, qcat[tp:].astype(F32).reshape(nb, ds, hw), c_s, kr_s.transpose(0, 2, 1),
                       wukt, g_kn.reshape(1, -1), cache_ckv, cache_krope_t, layer=layer)
    wuv = w_uv.transpose(1, 0, 2).astype(BF16)
    x_new = _mla_out(x, o_p, o_s, wuv, w_o.astype(BF16))
    return (x_new, ckv[:tp].reshape(b, l, KV_LORA), kr[:tp].reshape(b, l, ROPE_DIM), c_s, kr_s)


def kernel(x_prompt, x_sample, cache_ckv, cache_krope, state_s5_re, state_s5_im, page_table, norm_ffn1, ffn1_w_gate, ffn1_w_up, ffn1_w_down, norm_mix, norm_ffn2, ffn2_w_gate, ffn2_w_up, ffn2_w_down, ab_w_in, gm_norm_g, gm_norm_b, gm_w_s, gm_b_s, s5_lambda_re, s5_lambda_im, s5_log_dt, s5_b_re, s5_b_im, s5_c_re, s5_c_im, s5_d, s5_w_glu, s5_b_glu, ab_w_out, mla_w_a, mla_g_cq, mla_g_ckv, mla_w_uq, mla_g_qn, mla_g_qr, mla_g_kr, mla_w_uk, mla_g_kn, mla_w_uv, mla_w_o):
    b, l, d = x_prompt.shape
    nb, ds, _ = x_sample.shape
    depth = norm_ffn1.shape[0]
    past = page_table.shape[1] * cache_ckv.shape[2]
    x = jnp.concatenate([x_prompt.reshape(b * l, d), x_sample.reshape(nb * ds, d)], axis=0)
    cos, sin = _rope_tables(b, l, nb, ds, past)
    cache_krope_t = jnp.swapaxes(cache_krope, 2, 3)
    ckv_p, kr_p, ckv_s, kr_s = [], [], [], []
    s5r_p, s5i_p, s5r_s, s5i_s, gmv_s = [], [], [], [], []
    for layer in range(depth):
        x = _ffn(x, norm_ffn1[layer], ffn1_w_gate[layer].astype(BF16), ffn1_w_up[layer].astype(BF16),
                 ffn1_w_down[layer].astype(BF16))
        if layer % 2 == 0:
            i = layer // 2
            x, v_s, hr_p, hi_p, hr_s, hi_s = _even_layer(
                x, b, l, nb, ds, state_s5_re[i], state_s5_im[i], norm_mix[layer], ab_w_in[i],
                gm_norm_g[i], gm_norm_b[i], gm_w_s[i], gm_b_s[i], s5_lambda_re[i], s5_lambda_im[i],
                s5_log_dt[i], s5_b_re[i], s5_b_im[i], s5_c_re[i], s5_c_im[i], s5_d[i], s5_w_glu[i],
                s5_b_glu[i], ab_w_out[i])
            s5r_p.append(hr_p)
            s5i_p.append(hi_p)
            s5r_s.append(hr_s)
            s5i_s.append(hi_s)
            gmv_s.append(v_s)
        else:
            j = layer // 2
            x, c_p, k_p, c_s, k_s = _mla_layer(
                x, b, l, nb, ds, j, cos, sin, norm_mix[layer], cache_ckv, cache_krope_t, page_table,
                mla_w_a[j], mla_g_cq[j], mla_g_ckv[j], mla_w_uq[j], mla_g_qn[j], mla_g_qr[j],
                mla_g_kr[j], mla_w_uk[j], mla_g_kn[j], mla_w_uv[j], mla_w_o[j])
            ckv_p.append(c_p)
            kr_p.append(k_p)
            ckv_s.append(c_s)
            kr_s.append(k_s)
        x = _ffn(x, norm_ffn2[layer], ffn2_w_gate[layer].astype(BF16), ffn2_w_up[layer].astype(BF16),
                 ffn2_w_down[layer].astype(BF16))
    xp = x[:b * l].reshape(b, l, d)
    xs = x[b * l:].reshape(nb, ds, d)
    return (xp, xs,
            jnp.stack(ckv_p), jnp.stack(kr_p), jnp.stack(ckv_s), jnp.stack(kr_s),
            jnp.stack(s5r_p), jnp.stack(s5i_p), jnp.stack(s5r_s), jnp.stack(s5i_s),
            jnp.stack(gmv_s))
```

```python
import functools
import math

import jax
import jax.numpy as jnp
import numpy as np
from jax import lax
from jax.experimental import pallas as pl
from jax.experimental.pallas import tpu as pltpu

F32 = jnp.float32
BF16 = jnp.bfloat16

EPS = 1e-6
GM_HEADS = 4
CHUNK = 128
S5_CH = 16
S5_STATE = 64
S5_QCHUNKS = 4
MLA_HEADS = 8
Q_LORA = 384
KV_LORA = 256
NOPE_DIM = 128
ROPE_DIM = 64
ROPE_HALF = ROPE_DIM // 2
V_DIM = 128
ROPE_BASE = 10000.0
ATTN_SCALE = (NOPE_DIM + ROPE_DIM) ** -0.5
Q_SCALE = ATTN_SCALE * math.log2(math.e)
HEAD_PAD = 256
LANES = 128
SUBLANES = 8
VMEM_LIMIT = 56 * 1024 * 1024


def _pick(n, prefs):
    for p in prefs:
        if n % p == 0:
            return p
    raise ValueError(f"no tile in {prefs} divides {n}")


def _params(*sem):
    return pltpu.CompilerParams(dimension_semantics=sem, vmem_limit_bytes=VMEM_LIMIT)


def _const_spec(shape):
    n = len(shape)
    return pl.BlockSpec(shape, lambda *_: (0,) * n)


def _rms(x, g):
    return x * lax.rsqrt(jnp.mean(x * x, axis=-1, keepdims=True) + EPS) * g


def _sigmoid(x):
    return 1.0 / (1.0 + jnp.exp(-x))


def _gelu_tanh(x):
    return 0.5 * x * (1.0 + jnp.tanh(math.sqrt(2.0 / math.pi) * (x + 0.044715 * (x * x * x))))


def _dot(a, b):
    return jnp.dot(a, b, preferred_element_type=F32)


def _lane_tile(x, n):
    return jnp.concatenate([x] * n, axis=1)


def _dot_nt(a, b):
    return lax.dot_general(a, b, (((1,), (1,)), ((), ())), preferred_element_type=F32)


def _ffn_body(*refs, fchunk, n_in, n_out, n_head_tiles):
    x_refs = refs[:n_in]
    g_ref, wg_ref, wu_ref, wd_ref = refs[n_in:n_in + 4]
    o_refs = refs[n_in + 4:n_in + 4 + n_out]
    a_scr = refs[-1]
    i = pl.program_id(0)
    if n_in == 1:
        x = x_refs[0][...]
    else:
        x = jnp.where(i < n_head_tiles, x_refs[0][...], x_refs[1][...])
    h = _rms(x, g_ref[...]).astype(BF16)
    d_ff = wg_ref.shape[1]
    for c in range(0, d_ff, fchunk):
        g = _dot(h, wg_ref[:, c:c + fchunk])
        u = _dot(h, wu_ref[:, c:c + fchunk])
        a_scr[:, c:c + fchunk] = (g * _sigmoid(g) * u).astype(BF16)
    y = x + 0.5 * _dot(a_scr[...], wd_ref[...])
    if n_out == 1:
        o_refs[0][...] = y
    else:
        @pl.when(i < n_head_tiles)
        def _():
            o_refs[0][...] = y

        @pl.when(i >= n_head_tiles)
        def _():
            o_refs[1][...] = y


def _ffn(xs, g, wg, wu, wd, split_rows=None):
    xs = xs if isinstance(xs, tuple) else (xs,)
    d = xs[0].shape[1]
    t = sum(x.shape[0] for x in xs)
    d_ff = wg.shape[1]
    head = xs[0].shape[0] if len(xs) == 2 else (split_rows or t)
    tm = _pick(math.gcd(head, t - head) if head < t else t, (512, 256, 128, 64, 32, 16, 8))
    n_head = head // tm
    head_map = lambda i: (jnp.minimum(i, n_head - 1), 0)
    tail_map = lambda i: (jnp.maximum(i - n_head, 0), 0)
    row_map = lambda i: (i, 0)
    x_specs = ([pl.BlockSpec((tm, d), row_map)] if len(xs) == 1 else
               [pl.BlockSpec((tm, d), head_map), pl.BlockSpec((tm, d), tail_map)])
    if split_rows is None:
        out_shape = jax.ShapeDtypeStruct((t, d), F32)
        out_specs = pl.BlockSpec((tm, d), row_map)
    else:
        out_shape = (jax.ShapeDtypeStruct((head, d), F32), jax.ShapeDtypeStruct((t - head, d), F32))
        out_specs = (pl.BlockSpec((tm, d), head_map), pl.BlockSpec((tm, d), tail_map))
    return pl.pallas_call(
        functools.partial(_ffn_body, fchunk=256, n_in=len(xs), n_out=1 if split_rows is None else 2,
                          n_head_tiles=n_head),
        out_shape=out_shape,
        grid=(t // tm,),
        in_specs=x_specs + [
            _const_spec((1, d)),
            _const_spec((d, d_ff)),
            _const_spec((d, d_ff)),
            _const_spec((d_ff, d)),
        ],
        out_specs=out_specs,
        scratch_shapes=[pltpu.VMEM((tm, d_ff), BF16)],
        compiler_params=_params("arbitrary"),
        name="ffn",
    )(*xs, g.reshape(1, d), wg, wu, wd)


def _s5_tables(lam_re, lam_im, log_dt, b_re, b_im, c_re, c_im):
    lr, li = lam_re.astype(F32), lam_im.astype(F32)
    dt = jnp.exp(log_dt.astype(F32))[:, None]
    mag = jnp.exp(lr * dt)
    ar, ai = mag * jnp.cos(li * dt), mag * jnp.sin(li * dt)
    er, ei = ar - 1.0, ai
    den = lr * lr + li * li
    qr = (er * lr + ei * li) / den
    qi = (ei * lr - er * li) / den
    br, bi = b_re.astype(F32), b_im.astype(F32)
    bbr = qr[..., None] * br - qi[..., None] * bi
    bbi = qr[..., None] * bi + qi[..., None] * br
    g = lr.shape[0]
    gq = g // S5_QCHUNKS
    eye = jnp.eye(gq, dtype=F32)

    def b_blk(bb):
        bq = bb.reshape(S5_QCHUNKS, gq, S5_STATE, S5_CH)
        return jnp.einsum('qgnc,gh->qhcgn', bq, eye).reshape(S5_QCHUNKS, gq * S5_CH, gq * S5_STATE)

    def c_blk(cc):
        cq = cc.astype(F32).reshape(S5_QCHUNKS, gq, S5_CH, S5_STATE)
        return jnp.einsum('qgcn,gh->qgnhc', cq, eye).reshape(S5_QCHUNKS, gq * S5_STATE, gq * S5_CH)

    bblk = jnp.concatenate([b_blk(bbr), b_blk(bbi)], axis=2).astype(BF16)
    cblk = jnp.concatenate([c_blk(c_re), -c_blk(c_im)], axis=1).astype(BF16)
    a_tab = jnp.stack([ar.reshape(S5_QCHUNKS, -1), ai.reshape(S5_QCHUNKS, -1)], axis=1)
    return a_tab, bblk, cblk


def _pack_state(hr, hi):
    n = hr.shape[0]
    r = hr.astype(F32).reshape(n, S5_QCHUNKS, 1, -1)
    i = hi.astype(F32).reshape(n, S5_QCHUNKS, 1, -1)
    return jnp.concatenate([r, i], axis=2).reshape(n, -1)


def _unpack_state(st, groups):
    n = st.shape[0]
    s4 = st.reshape(n, S5_QCHUNKS, 2, -1)
    return (s4[:, :, 0].reshape(n, groups, S5_STATE), s4[:, :, 1].reshape(n, groups, S5_STATE))


def _cmul(ar, ai, br, bi):
    return ar * br - ai * bi, ar * bi + ai * br


def _even_prompt_body(x_ref, gmix_ref, win_ref, lng_ref, lnb_ref, wtril_ref, sbias_ref,
                      atab_ref, bblk_ref, cblk_ref, d_ref, wglu_ref, bglu_ref, wout_ref,
                      o_ref, st_ref,
                      us_scr, usp_scr, xs_scr, yp_scr, y_scr, cat_scr, carry_scr, *, tt):
    m = tt // SUBLANES
    gw = lng_ref.shape[1]
    sw = d_ref.shape[1]
    hs = sw
    t_idx = pl.program_id(1)

    @pl.when(t_idx == 0)
    def _():
        carry_scr[...] = jnp.zeros_like(carry_scr)

    x = x_ref[...]
    hn = _rms(x, gmix_ref[...]).astype(BF16)
    z = _dot(hn, win_ref[...])
    u = _gelu_tanh(z[:, :gw])
    v = _gelu_tanh(z[:, gw:2 * gw])
    mu = jnp.mean(v, axis=-1, keepdims=True)
    vc = v - mu
    v = vc * lax.rsqrt(jnp.mean(vc * vc, axis=-1, keepdims=True) + EPS) * lng_ref[...] + lnb_ref[...]
    vb = v.astype(BF16)
    hd = gw // GM_HEADS
    for c in range(tt // CHUNK):
        rows = slice(c * CHUNK, (c + 1) * CHUNK)
        for h in range(GM_HEADS):
            cols = slice(h * hd, (h + 1) * hd)
            s = _dot(wtril_ref[h], vb[rows, cols]) + sbias_ref[:, cols]
            cat_scr[rows, cols] = (u[rows, cols] * s).astype(BF16)

    qw = sw // S5_QCHUNKS
    for q in range(S5_QCHUNKS):
        us_scr[q] = z[:, 2 * gw + q * qw:2 * gw + (q + 1) * qw]
    sub = lax.broadcasted_iota(jnp.int32, (SUBLANES, hs), 0)

    def shift_rows(val, k):
        return jnp.where(sub >= k, pltpu.roll(val, k, axis=0), 0.0)

    for q in range(S5_QCHUNKS):
        for i in range(m):
            usp_scr[i * SUBLANES:(i + 1) * SUBLANES, :] = us_scr.at[q][pl.ds(i, SUBLANES, stride=m), :]
        xs_scr[...] = _dot(usp_scr[...].astype(BF16), bblk_ref[q])
        ar = atab_ref[q, 0:1, :]
        ai = atab_ref[q, 1:2, :]
        arb = jnp.broadcast_to(ar, (SUBLANES, hs))
        aib = jnp.broadcast_to(ai, (SUBLANES, hs))

        def step(i, carry, store):
            hr, hi = carry
            r0 = pl.multiple_of(i * SUBLANES, SUBLANES)
            xr = xs_scr[pl.ds(r0, SUBLANES), 0:hs]
            xi = xs_scr[pl.ds(r0, SUBLANES), hs:2 * hs]
            nr = arb * hr - aib * hi + xr
            ni = arb * hi + aib * hr + xi
            if store:
                xs_scr[pl.ds(r0, SUBLANES), 0:hs] = nr
                xs_scr[pl.ds(r0, SUBLANES), hs:2 * hs] = ni
            return nr, ni

        zero = jnp.zeros((SUBLANES, hs), F32)
        er, ei = lax.fori_loop(0, m, functools.partial(step, store=False), (zero, zero), unroll=True)

        pr, pi = ar, ai
        for _ in range(int(math.log2(m))):
            pr, pi = _cmul(pr, pi, pr, pi)
        a1 = (pr, pi)
        a2 = _cmul(*a1, *a1)
        a4 = _cmul(*a2, *a2)
        fr, fi = er, ei
        for k, (kr, ki) in ((1, a1), (2, a2), (4, a4)):
            sr, si = shift_rows(fr, k), shift_rows(fi, k)
            dr, di = _cmul(kr, ki, sr, si)
            fr, fi = fr + dr, fi + di
        wr, wi = jnp.ones((SUBLANES, hs), F32), jnp.zeros((SUBLANES, hs), F32)
        for bit, (kr, ki) in ((1, a1), (2, a2), (4, a4)):
            sel = (sub & bit) != 0
            wr, wi = _cmul(wr, wi, jnp.where(sel, kr, 1.0), jnp.where(sel, ki, 0.0))
        cr = carry_scr[q, SUBLANES - 1:SUBLANES, 0:hs]
        ci = carry_scr[q, SUBLANES - 1:SUBLANES, hs:2 * hs]
        tr, ti = _cmul(wr, wi, cr, ci)
        h0r = tr + shift_rows(fr, 1)
        h0i = ti + shift_rows(fi, 1)
        nr, ni = _cmul(a1[0], a1[1], h0r, h0i)
        carry_scr[q, :, 0:hs] = nr + er
        carry_scr[q, :, hs:2 * hs] = ni + ei
        st_ref[0, :, q * 2 * hs:(q + 1) * 2 * hs] = carry_scr[q, SUBLANES - 1:SUBLANES, :]

        lax.fori_loop(0, m, functools.partial(step, store=True), (h0r, h0i), unroll=True)
        yp_scr[...] = _dot(xs_scr[...].astype(BF16), cblk_ref[q])
        for s in range(SUBLANES):
            y_scr[s * m:(s + 1) * m, q * qw:(q + 1) * qw] = (
                yp_scr[pl.ds(s, m, stride=SUBLANES), :]
                + d_ref[:, q * qw:(q + 1) * qw] * us_scr[q, s * m:(s + 1) * m, :])

    g = _gelu_tanh(y_scr[...])
    gate = _sigmoid(_dot(g.astype(BF16), wglu_ref[...]) + bglu_ref[...])
    cat_scr[:, gw:gw + sw] = (g * gate).astype(BF16)
    o_ref[...] = x + _dot(cat_scr[...], wout_ref[...])


def _even_prompt(x, b, l, gmix, win, lng, lnb, wtril, sbias, atab, bblk, cblk, dvec, wglu, bglu, wout):
    d = x.shape[1]
    gw, sw = lng.shape[1], dvec.shape[1]
    tt = _pick(l, (256, 128))
    nt = l // tt
    hs2 = 2 * sw
    args = (x, gmix, win, lng, lnb, wtril, sbias, atab, bblk, cblk, dvec, wglu, bglu, wout)
    in_specs = [pl.BlockSpec((tt, d), lambda bi, ti: (bi * nt + ti, 0))]
    in_specs += [_const_spec(a.shape) for a in args[1:]]
    return pl.pallas_call(
        functools.partial(_even_prompt_body, tt=tt),
        out_shape=(jax.ShapeDtypeStruct((b * l, d), F32),
                   jax.ShapeDtypeStruct((b, 1, S5_QCHUNKS * hs2), F32)),
        grid=(b, nt),
        in_specs=in_specs,
        out_specs=(pl.BlockSpec((tt, d), lambda bi, ti: (bi * nt + ti, 0)),
                   pl.BlockSpec((1, 1, S5_QCHUNKS * hs2), lambda bi, ti: (bi, 0, 0))),
        scratch_shapes=[
            pltpu.VMEM((S5_QCHUNKS, tt, sw // S5_QCHUNKS), F32),
            pltpu.VMEM((tt, sw // S5_QCHUNKS), F32),
            pltpu.VMEM((tt, hs2), F32),
            pltpu.VMEM((tt, sw // S5_QCHUNKS), F32),
            pltpu.VMEM((tt, sw), F32),
            pltpu.VMEM((tt, gw + sw), BF16),
            pltpu.VMEM((S5_QCHUNKS, SUBLANES, hs2), F32),
        ],
        compiler_params=_params("arbitrary", "arbitrary"),
        name="even_prompt",
    )(*args)


def _even_sample_body(x_ref, h0_ref, gmix_ref, win_ref, lng_ref, lnb_ref, scoef_ref, sbias_ref,
                      atab_ref, bblk_ref, cblk_ref, d_ref, wglu_ref, bglu_ref, wout_ref,
                      o_ref, v_ref, st_ref,
                      xs_scr, y_scr, cat_scr, *, ds, nb):
    gw = lng_ref.shape[1]
    sw = d_ref.shape[1]
    hs = sw
    x = x_ref[...]
    hn = _rms(x, gmix_ref[...]).astype(BF16)
    z = _dot(hn, win_ref[...])
    u = _gelu_tanh(z[:, :gw])
    v = _gelu_tanh(z[:, gw:2 * gw])
    mu = jnp.mean(v, axis=-1, keepdims=True)
    vc = v - mu
    v = vc * lax.rsqrt(jnp.mean(vc * vc, axis=-1, keepdims=True) + EPS) * lng_ref[...] + lnb_ref[...]
    v_ref[...] = v
    for t in range(ds):
        s = jnp.broadcast_to(sbias_ref[t:t + 1, :], (nb, gw))
        for sp in range(t + 1):
            s = s + scoef_ref[t * ds + sp:t * ds + sp + 1, :] * v[sp * nb:(sp + 1) * nb, :]
        cat_scr[t * nb:(t + 1) * nb, 0:gw] = (u[t * nb:(t + 1) * nb, :] * s).astype(BF16)

    us = z[:, 2 * gw:]
    usb = us.astype(BF16)
    qw = sw // S5_QCHUNKS
    for q in range(S5_QCHUNKS):
        xs_scr[...] = _dot(usb[:, q * qw:(q + 1) * qw], bblk_ref[q])
        ar = atab_ref[q, 0:1, :]
        ai = atab_ref[q, 1:2, :]
        hr = h0_ref[:, q * 2 * hs:q * 2 * hs + hs]
        hi = h0_ref[:, q * 2 * hs + hs:(q + 1) * 2 * hs]
        for t in range(ds):
            rows = slice(t * nb, (t + 1) * nb)
            nr = ar * hr - ai * hi + xs_scr[rows, 0:hs]
            ni = ar * hi + ai * hr + xs_scr[rows, hs:2 * hs]
            xs_scr[rows, 0:hs] = nr
            xs_scr[rows, hs:2 * hs] = ni
            hr, hi = nr, ni
        st_ref[:, q * 2 * hs:q * 2 * hs + hs] = hr
        st_ref[:, q * 2 * hs + hs:(q + 1) * 2 * hs] = hi
        y_scr[:, q * qw:(q + 1) * qw] = _dot(xs_scr[...].astype(BF16), cblk_ref[q])
    g = _gelu_tanh(y_scr[...] + d_ref[...] * us)
    gate = _sigmoid(_dot(g.astype(BF16), wglu_ref[...]) + bglu_ref[...])
    cat_scr[:, gw:gw + sw] = (g * gate).astype(BF16)
    o_ref[...] = x + _dot(cat_scr[...], wout_ref[...])


def _even_sample(xs_tm, h0, gmix, win, lng, lnb, scoef, sbias8, atab, bblk, cblk, dvec, wglu, bglu, wout, *, ds):
    t, d = xs_tm.shape
    nb = t // ds
    gw, sw = lng.shape[1], dvec.shape[1]
    args = (xs_tm, h0, gmix, win, lng, lnb, scoef, sbias8, atab, bblk, cblk, dvec, wglu, bglu, wout)
    return pl.pallas_call(
        functools.partial(_even_sample_body, ds=ds, nb=nb),
        out_shape=(jax.ShapeDtypeStruct((t, d), F32),
                   jax.ShapeDtypeStruct((t, gw), F32),
                   jax.ShapeDtypeStruct(h0.shape, F32)),
        grid=(1,),
        in_specs=[_const_spec(a.shape) for a in args],
        out_specs=(_const_spec((t, d)), _const_spec((t, gw)), _const_spec(h0.shape)),
        scratch_shapes=[
            pltpu.VMEM((t, 2 * sw), F32),
            pltpu.VMEM((t, sw), F32),
            pltpu.VMEM((t, gw + sw), BF16),
        ],
        compiler_params=_params("arbitrary"),
        name="even_sample",
    )(*args)


def _mla_proj_body(x_ref, cos_ref, sin_ref, gmix_ref, wa_ref, gcq_ref, gckv_ref, gkr_ref,
                   wuq_ref, gqn_ref, gq1_ref, gq2_ref, wuk_ref, gkn_ref, ones_ref, perm_ref,
                   ckv_ref, kr_ref, cbf_ref, qcat_ref, kcat_ref):
    tm = x_ref.shape[0]
    x = x_ref[...]
    hn = _rms(x, gmix_ref[...]).astype(BF16)
    a = _dot(hn, wa_ref[...])
    cq = _rms(a[:, :Q_LORA], gcq_ref[...])
    ckv = _rms(a[:, Q_LORA:Q_LORA + KV_LORA], gckv_ref[...])
    ckv_ref[...] = ckv
    ckv_b = ckv.astype(BF16)
    cbf_ref[...] = ckv_b

    cos = cos_ref[...]
    sin = sin_ref[...]
    slab = a[:, Q_LORA + KV_LORA:]
    ms = jnp.sum(slab * slab, axis=-1, keepdims=True) * (1.0 / ROPE_DIM)
    kn = slab * lax.rsqrt(ms + EPS) * gkr_ref[...]
    lane = lax.broadcasted_iota(jnp.int32, (tm, LANES), 1)
    rot = jnp.where(lane < ROPE_HALF, -pltpu.roll(kn, LANES - ROPE_HALF, axis=1),
                    pltpu.roll(kn, ROPE_HALF, axis=1))
    kr = kn * cos + rot * sin
    kr_ref[...] = kr[:, :ROPE_DIM]
    kr_pad = jnp.where(lane < ROPE_DIM, kr, 0.0).astype(BF16)

    q = _dot(cq.astype(BF16), wuq_ref[...])
    k = _dot(ckv_b, wuk_ref[...])
    nn = MLA_HEADS * NOPE_DIM
    rw = MLA_HEADS * ROPE_HALF
    x1 = q[:, nn:nn + rw]
    x2 = q[:, nn + rw:nn + 2 * rw]
    ssq = x1 * x1 + x2 * x2
    ssq_hi = ssq.astype(BF16)
    ssq_lo = (ssq - ssq_hi.astype(F32)).astype(BF16)
    gsum = _dot(ssq_hi, ones_ref[...]) + _dot(ssq_lo, ones_ref[...])
    r = lax.rsqrt(gsum * (1.0 / ROPE_DIM) + EPS)
    x1n = x1 * r * gq1_ref[...]
    x2n = x2 * r * gq2_ref[...]
    reps = rw // LANES
    cos2 = jnp.concatenate([cos] * reps, axis=1)
    sin2 = jnp.concatenate([sin] * reps, axis=1)
    o1 = (x1n * cos2 - x2n * sin2) * Q_SCALE
    o2 = (x2n * cos2 + x1n * sin2) * Q_SCALE
    qr_cat = _dot(jnp.concatenate([o1, o2], axis=1).astype(BF16), perm_ref[...]).astype(BF16)
    for h in range(MLA_HEADS):
        nsl = slice(h * NOPE_DIM, (h + 1) * NOPE_DIM)
        qcat_ref[:, h * HEAD_PAD:h * HEAD_PAD + NOPE_DIM] = (_rms(q[:, nsl], gqn_ref[...]) * Q_SCALE).astype(BF16)
        qcat_ref[:, h * HEAD_PAD + NOPE_DIM:(h + 1) * HEAD_PAD] = qr_cat[:, h * LANES:(h + 1) * LANES]
        kcat_ref[:, h * HEAD_PAD:h * HEAD_PAD + NOPE_DIM] = _rms(k[:, nsl], gkn_ref[...]).astype(BF16)
        kcat_ref[:, h * HEAD_PAD + NOPE_DIM:(h + 1) * HEAD_PAD] = kr_pad


def _mla_proj(x, cos, sin, consts):
    t, d = x.shape
    tm = _pick(t, (256, 128, 64, 8))
    hw = MLA_HEADS * HEAD_PAD
    row = lambda w: pl.BlockSpec((tm, w), lambda i: (i, 0))
    return pl.pallas_call(
        _mla_proj_body,
        out_shape=(jax.ShapeDtypeStruct((t, KV_LORA), F32),
                   jax.ShapeDtypeStruct((t, ROPE_DIM), F32),
                   jax.ShapeDtypeStruct((t, KV_LORA), BF16),
                   jax.ShapeDtypeStruct((t, hw), BF16),
                   jax.ShapeDtypeStruct((t, hw), BF16)),
        grid=(t // tm,),
        in_specs=[row(d), row(LANES), row(LANES)] + [_const_spec(c.shape) for c in consts],
        out_specs=(row(KV_LORA), row(ROPE_DIM), row(KV_LORA), row(hw), row(hw)),
        compiler_params=_params("arbitrary"),
        name="mla_proj",
    )(x, cos, sin, *consts)


def _attn_prompt_body(qi_ref, kj_ref, q_ref, k_ref, c_ref, o_ref, m_scr, l_scr, acc_scr, *, tq):
    p = pl.program_id(1)
    qi = qi_ref[p]
    kj = kj_ref[p]

    @pl.when(kj == 0)
    def _():
        m_scr[...] = jnp.full_like(m_scr, -jnp.inf)
        l_scr[...] = jnp.zeros_like(l_scr)
        acc_scr[...] = jnp.zeros_like(acc_scr)

    def tile(sub, masked):
        ksl = slice(sub * tq, (sub + 1) * tq)
        c = c_ref[ksl, :]
        nrep = tq // LANES
        vrep = KV_LORA // LANES
        if masked:
            row = lax.broadcasted_iota(jnp.int32, (tq, tq), 0)
            col = lax.broadcasted_iota(jnp.int32, (tq, tq), 1)
            keep = row >= col
        for h in range(MLA_HEADS):
            hsl = slice(h * HEAD_PAD, (h + 1) * HEAD_PAD)
            s = _dot_nt(q_ref[:, hsl], k_ref[ksl, hsl])
            if masked:
                s = jnp.where(keep, s, -jnp.inf)
            m_prev = m_scr[h]
            m_new = jnp.maximum(m_prev, jnp.max(s, axis=-1, keepdims=True))
            alpha = jnp.exp2(m_prev - m_new)
            pexp = jnp.exp2(s - _lane_tile(m_new, nrep))
            psum = pexp[:, 0:LANES]
            for j in range(1, nrep):
                psum = psum + pexp[:, j * LANES:(j + 1) * LANES]
            l_scr[h] = alpha * l_scr[h] + psum
            acc_scr[h] = _lane_tile(alpha, vrep) * acc_scr[h] + _dot(pexp.astype(BF16), c)
            m_scr[h] = m_new

    def finalize():
        for h in range(MLA_HEADS):
            o_ref[:, h * KV_LORA:(h + 1) * KV_LORA] = (
                acc_scr[h] / jnp.sum(l_scr[h], axis=-1, keepdims=True)).astype(BF16)

    @pl.when(2 * kj + 1 < qi)
    def _():
        tile(0, False)
        tile(1, False)

    @pl.when(2 * kj + 1 == qi)
    def _():
        tile(0, False)
        tile(1, True)
        finalize()

    @pl.when(2 * kj == qi)
    def _():
        tile(0, True)
        finalize()


def _attn_prompt(qcat, kcat, cbf, b, l):
    hw = qcat.shape[1]
    tq = _pick(l, (512, 256, 128))
    nq = l // tq
    nkb = nq // 2
    pairs = [(i, j) for i in range(nq) for j in range(i // 2 + 1)]
    qi_tab = jnp.asarray([i for i, _ in pairs], jnp.int32)
    ki_tab = jnp.asarray([j for _, j in pairs], jnp.int32)
    grid_spec = pltpu.PrefetchScalarGridSpec(
        num_scalar_prefetch=2,
        grid=(b, len(pairs)),
        in_specs=[
            pl.BlockSpec((tq, hw), lambda bi, p, qi, kj: (bi * nq + qi[p], 0)),
            pl.BlockSpec((2 * tq, hw), lambda bi, p, qi, kj: (bi * nkb + kj[p], 0)),
            pl.BlockSpec((2 * tq, KV_LORA), lambda bi, p, qi, kj: (bi * nkb + kj[p], 0)),
        ],
        out_specs=pl.BlockSpec((tq, MLA_HEADS * KV_LORA), lambda bi, p, qi, kj: (bi * nq + qi[p], 0)),
        scratch_shapes=[
            pltpu.VMEM((MLA_HEADS, tq, LANES), F32),
            pltpu.VMEM((MLA_HEADS, tq, LANES), F32),
            pltpu.VMEM((MLA_HEADS, tq, KV_LORA), F32),
        ],
    )
    return pl.pallas_call(
        functools.partial(_attn_prompt_body, tq=tq),
        out_shape=jax.ShapeDtypeStruct((b * l, MLA_HEADS * KV_LORA), BF16),
        grid_spec=grid_spec,
        compiler_params=_params("arbitrary", "arbitrary"),
        name="attn_prompt",
    )(qi_tab, ki_tab, qcat, kcat, cbf)


def _attn_sample_body(pt_ref, q_ref, cn_ref, krn_ref, wukt_ref, gkn_ref, ckv_hbm, krc_hbm,
                      o_ref, cbuf, krbuf, sems, lhs_scr, big_scr, tail_c, tail_kr,
                      *, layer, n_pages, page, tk, ds):
    b = pl.program_id(0)
    nb = pl.num_programs(0)
    slot = lax.rem(b, 2)
    nh = MLA_HEADS
    nrow = nh * ds
    kw = nh * NOPE_DIM
    ppt = tk // page
    nt = n_pages // ppt

    def page_copies(seq, sl, j):
        pg = pt_ref[seq, j]
        r0 = pl.multiple_of(j * page, page)
        return (pltpu.make_async_copy(ckv_hbm.at[layer, pg], cbuf.at[sl, pl.ds(r0, page), :], sems.at[0, sl]),
                pltpu.make_async_copy(krc_hbm.at[layer, pg], krbuf.at[sl, j], sems.at[1, sl]))

    def start_fetch(seq, sl):
        def body(j, carry):
            for cp in page_copies(seq, sl, j):
                cp.start()
            return carry
        lax.fori_loop(0, n_pages, body, 0, unroll=8)

    @pl.when(b == 0)
    def _():
        start_fetch(0, 0)
        lhs_scr[0:kw, :] = wukt_ref[...]
        tail_c[...] = jnp.zeros_like(tail_c)
        tail_kr[...] = jnp.zeros_like(tail_kr)

    for j in range(n_pages):
        for cp in page_copies(b, slot, j):
            cp.wait()

    @pl.when(b + 1 < nb)
    def _():
        start_fetch(b + 1, 1 - slot)

    qf = q_ref[0]
    qabs, qrope = [], []
    for h in range(nh):
        qn = (qf[:, h * HEAD_PAD:h * HEAD_PAD + NOPE_DIM] * gkn_ref[...]).astype(BF16)
        qabs.append(_dot(qn, wukt_ref[h * NOPE_DIM:(h + 1) * NOPE_DIM, :]))
        qrope.append(qf[:, h * HEAD_PAD + NOPE_DIM:h * HEAD_PAD + NOPE_DIM + ROPE_DIM])
    lhs_scr[kw:kw + nrow, :] = jnp.concatenate(qabs, axis=0).astype(BF16)
    qr_b = jnp.concatenate(qrope, axis=0).astype(BF16)
    tail_c[0:ds, :] = cn_ref[0]
    tail_kr[:, 0:ds] = krn_ref[0]

    def softmax_update(k_rows, sraw, cb, krt, carry, width, masked):
        m_prev, l_prev, acc = carry
        rinv = []
        for h in range(nh):
            kt = k_rows(h)
            ss = jnp.sum(kt * kt, axis=0, keepdims=True) * (1.0 / NOPE_DIM)
            rinv.append(jnp.broadcast_to(lax.rsqrt(ss + EPS), (ds, width)))
        rinv = jnp.concatenate(rinv, axis=0)
        s = sraw * rinv + _dot(qr_b, krt.astype(BF16))
        if masked:
            qpos = lax.rem(lax.broadcasted_iota(jnp.int32, (nrow, width), 0), ds)
            kpos = lax.broadcasted_iota(jnp.int32, (nrow, width), 1)
            s = jnp.where(kpos <= qpos, s, -jnp.inf)
        m_new = jnp.maximum(m_prev, jnp.max(s, axis=-1, keepdims=True))
        alpha = jnp.exp2(m_prev - m_new)
        pexp = jnp.exp2(s - m_new)
        l_new = alpha * l_prev + jnp.sum(pexp, axis=-1, keepdims=True)
        acc = alpha * acc + _dot(pexp.astype(BF16), cb)
        return m_new, l_new, acc

    def c_tile(t):
        return cbuf[slot, t * tk:(t + 1) * tk, :].astype(BF16)

    def scores_into(t, sb):
        big_scr[sb] = _dot_nt(lhs_scr[...], c_tile(t))

    def consume(t, sb, carry):
        krt = jnp.concatenate([krbuf[slot, t * ppt + d] for d in range(ppt)], axis=1)
        return softmax_update(lambda h: big_scr[sb, h * NOPE_DIM:(h + 1) * NOPE_DIM, :],
                              big_scr[sb, kw:kw + nrow, :], c_tile(t), krt, carry, tk, False)

    carry = (jnp.full((nrow, 1), -jnp.inf, F32), jnp.zeros((nrow, 1), F32),
             jnp.zeros((nrow, KV_LORA), F32))
    scores_into(0, 0)
    for t in range(nt):
        if t + 1 < nt:
            scores_into(t + 1, (t + 1) % 2)
        carry = consume(t, t % 2, carry)
    tcb = tail_c[...].astype(BF16)
    tbig = _dot_nt(lhs_scr[...], tcb)
    _, l_fin, acc = softmax_update(lambda h: tbig[h * NOPE_DIM:(h + 1) * NOPE_DIM, :],
                                   tbig[kw:kw + nrow, :], tcb, tail_kr[...], carry,
                                   tail_c.shape[0], True)
    out = acc / l_fin
    for h in range(nh):
        o_ref[:, h * KV_LORA:(h + 1) * KV_LORA] = out[h * ds:(h + 1) * ds, :]


def _attn_sample(page_table, q_s, c_new, krt_new, wukt, gkn, cache_ckv, cache_krope_t, *, layer):
    nb, ds, hw = q_s.shape
    n_pages = page_table.shape[1]
    page = cache_ckv.shape[2]
    tk = _pick(n_pages * page, (512, 256, 128))
    nrow = MLA_HEADS * ds
    kw = MLA_HEADS * NOPE_DIM
    grid_spec = pltpu.PrefetchScalarGridSpec(
        num_scalar_prefetch=1,
        grid=(nb,),
        in_specs=[
            pl.BlockSpec((1, ds, hw), lambda i, pt: (i, 0, 0)),
            pl.BlockSpec((1, ds, KV_LORA), lambda i, pt: (i, 0, 0)),
            pl.BlockSpec((1, ROPE_DIM, ds), lambda i, pt: (i, 0, 0)),
            pl.BlockSpec((kw, KV_LORA), lambda i, pt: (0, 0)),
            pl.BlockSpec((1, NOPE_DIM), lambda i, pt: (0, 0)),
            pl.BlockSpec(memory_space=pl.ANY),
            pl.BlockSpec(memory_space=pl.ANY),
        ],
        out_specs=pl.BlockSpec((ds, MLA_HEADS * KV_LORA), lambda i, pt: (i, 0)),
        scratch_shapes=[
            pltpu.VMEM((2, n_pages * page, KV_LORA), F32),
            pltpu.VMEM((2, n_pages, ROPE_DIM, page), F32),
            pltpu.SemaphoreType.DMA((2, 2)),
            pltpu.VMEM((kw + nrow, KV_LORA), BF16),
            pltpu.VMEM((2, kw + nrow, tk), F32),
            pltpu.VMEM((LANES, KV_LORA), F32),
            pltpu.VMEM((ROPE_DIM, LANES), F32),
        ],
    )
    return pl.pallas_call(
        functools.partial(_attn_sample_body, layer=layer, n_pages=n_pages, page=page, tk=tk, ds=ds),
        out_shape=jax.ShapeDtypeStruct((nb * ds, MLA_HEADS * KV_LORA), F32),
        grid_spec=grid_spec,
        compiler_params=_params("arbitrary"),
        name="attn_sample",
    )(page_table, q_s, c_new, krt_new, wukt, gkn, cache_ckv, cache_krope_t)


def _mla_out_body(x_ref, op_ref, os_ref, wuv_ref, wo_ref, out_ref, v_scr, *, n_prompt_tiles):
    i = pl.program_id(0)

    def up_project(o_ref):
        for h in range(MLA_HEADS):
            v_scr[:, h * V_DIM:(h + 1) * V_DIM] = _dot(
                o_ref[:, h * KV_LORA:(h + 1) * KV_LORA].astype(BF16), wuv_ref[h]).astype(BF16)

    @pl.when(i < n_prompt_tiles)
    def _():
        up_project(op_ref)

    @pl.when(i >= n_prompt_tiles)
    def _():
        up_project(os_ref)

    out_ref[...] = x_ref[...] + _dot(v_scr[...], wo_ref[...])


def _mla_out(x, olat_p, olat_s, wuv, wo):
    t, d = x.shape
    tp, ow = olat_p.shape
    ts = olat_s.shape[0]
    tm = _pick(math.gcd(tp, ts), (512, 256, 128, 64, 32, 16, 8))
    n_p = tp // tm
    return pl.pallas_call(
        functools.partial(_mla_out_body, n_prompt_tiles=n_p),
        out_shape=jax.ShapeDtypeStruct((t, d), F32),
        grid=(t // tm,),
        in_specs=[pl.BlockSpec((tm, d), lambda i: (i, 0)),
                  pl.BlockSpec((tm, ow), lambda i: (jnp.minimum(i, n_p - 1), 0)),
                  pl.BlockSpec((tm, ow), lambda i: (jnp.maximum(i - n_p, 0), 0)),
                  _const_spec(wuv.shape), _const_spec(wo.shape)],
        out_specs=pl.BlockSpec((tm, d), lambda i: (i, 0)),
        scratch_shapes=[pltpu.VMEM((tm, MLA_HEADS * V_DIM), BF16)],
        compiler_params=_params("arbitrary"),
        name="mla_out",
    )(x, olat_p, olat_s, wuv, wo)


def _even_layer(x, b, l, nb, ds, h0r, h0i, gmix, w_in, gm_g, gm_b, gm_w_s, gm_b_s,
                lam_re, lam_im, log_dt, b_re, b_im, c_re, c_im, dvec, w_glu, b_glu, w_out):
    d = x.shape[1]
    gw = gm_g.shape[0]
    sw = dvec.shape[0]
    groups = lam_re.shape[0]
    atab, bblk, cblk = _s5_tables(lam_re, lam_im, log_dt, b_re, b_im, c_re, c_im)
    causal = jnp.tril(jnp.ones((CHUNK, CHUNK), dtype=bool))
    w_tril = jnp.where(causal[None], gm_w_s, 0.0)
    sbias = jnp.repeat(gm_b_s.T, gw // GM_HEADS, axis=1)
    scoef = jnp.repeat(w_tril[:, :ds, :ds].transpose(1, 2, 0).reshape(ds * ds, GM_HEADS),
                       gw // GM_HEADS, axis=1)
    common = (gmix.reshape(1, d), w_in.astype(BF16), gm_g.reshape(1, gw), gm_b.reshape(1, gw))
    tail = (atab, bblk, cblk, dvec.reshape(1, sw), w_glu.astype(BF16), b_glu.reshape(1, sw),
            w_out.astype(BF16))
    yp, st_p = _even_prompt(x, b, l, *common, w_tril.astype(BF16), sbias, *tail)
    xs_tm = x[b * l:].reshape(nb, ds, d).transpose(1, 0, 2).reshape(ds * nb, d)
    ys_tm, v_tm, st_s = _even_sample(xs_tm, _pack_state(h0r, h0i), *common, scoef, sbias[:ds], *tail, ds=ds)
    ys = ys_tm.reshape(ds, nb, d).transpose(1, 0, 2).reshape(nb * ds, d)
    v_s = v_tm.reshape(ds, nb, gw).transpose(1, 0, 2)
    x_new = jnp.concatenate([yp, ys], axis=0)
    hr_p, hi_p = _unpack_state(st_p.reshape(b, -1), groups)
    hr_s, hi_s = _unpack_state(st_s, groups)
    return x_new, v_s, hr_p, hi_p, hr_s, hi_s


def _rope_tables(b, l, nb, ds, past):
    inv = ROPE_BASE ** (-jnp.arange(ROPE_HALF, dtype=F32) * (2.0 / ROPE_DIM))
    pos_p = jnp.arange(l, dtype=jnp.int32)
    pos_s = past + jnp.arange(ds, dtype=jnp.int32)
    pos = jnp.concatenate([jnp.tile(pos_p, b), jnp.tile(pos_s, nb)])
    ang = pos.astype(F32)[:, None] * inv[None, :]
    reps = LANES // ROPE_HALF
    return jnp.tile(jnp.cos(ang), (1, reps)), jnp.tile(jnp.sin(ang), (1, reps))


def _mla_layer(x, b, l, nb, ds, layer, cos, sin, gmix, cache_ckv, cache_krope_t, page_table,
               w_a, g_cq, g_ckv, w_uq, g_qn, g_qr, g_kr, w_uk, g_kn, w_uv, w_o):
    d = x.shape[1]
    nh = MLA_HEADS
    wa_pad = jnp.concatenate([w_a, jnp.zeros((d, LANES - ROPE_DIM), w_a.dtype)], axis=1).astype(BF16)
    wq3 = w_uq.reshape(Q_LORA, nh, NOPE_DIM + ROPE_DIM)
    wuq = jnp.concatenate([wq3[:, :, :NOPE_DIM].reshape(Q_LORA, -1),
                           wq3[:, :, NOPE_DIM:NOPE_DIM + ROPE_HALF].reshape(Q_LORA, -1),
                           wq3[:, :, NOPE_DIM + ROPE_HALF:].reshape(Q_LORA, -1)], axis=1).astype(BF16)
    wuk = w_uk.reshape(KV_LORA, nh * NOPE_DIM).astype(BF16)
    rw = nh * ROPE_HALF
    grp = np.arange(rw) // ROPE_HALF
    ones_bd = jnp.asarray(grp[:, None] == grp[None, :], BF16)
    perm = np.zeros((2 * rw, nh * LANES), np.float32)
    for h in range(nh):
        for j in range(ROPE_HALF):
            perm[h * ROPE_HALF + j, h * LANES + j] = 1.0
            perm[rw + h * ROPE_HALF + j, h * LANES + ROPE_HALF + j] = 1.0
    gkr_pad = jnp.concatenate([g_kr, jnp.zeros((LANES - ROPE_DIM,), g_kr.dtype)]).reshape(1, LANES)
    consts = (gmix.reshape(1, d), wa_pad, g_cq.reshape(1, -1), g_ckv.reshape(1, -1), gkr_pad,
              wuq, g_qn.reshape(1, -1), jnp.tile(g_qr[:ROPE_HALF], nh).reshape(1, rw),
              jnp.tile(g_qr[ROPE_HALF:], nh).reshape(1, rw), wuk, g_kn.reshape(1, -1),
              ones_bd, jnp.asarray(perm, BF16))
    ckv, kr, cbf, qcat, kcat = _mla_proj(x, cos, sin, consts)
    tp = b * l
    hw = nh * HEAD_PAD
    o_p = _attn_prompt(qcat, kcat, cbf, b, l)
    c_s = ckv[tp:].reshape(nb, ds, KV_LORA)
    kr_s = kr[tp:].reshape(nb, ds, ROPE_DIM)
    wukt = w_uk.reshape(KV_LORA, nh * NOPE_DIM).T.astype(BF16)
    o_s = _attn_sample(page_table, qcat[tp:].astype(F32).reshape(nb, ds, hw), c_s, kr_s.transpose(0, 2, 1),
                       wukt, g_kn.reshape(1, -1), cache_ckv, cache_krope_t, layer=layer)
    wuv = w_uv.transpose(1, 0, 2).astype(BF16)
    x_new = _mla_out(x, o_p, o_s, wuv, w_o.astype(BF16))
    return (x_new, ckv[:tp].reshape(b, l, KV_LORA), kr[:tp].reshape(b, l, ROPE_DIM), c_s, kr_s)


def kernel(x_prompt, x_sample, cache_ckv, cache_krope, state_s5_re, state_s5_im, page_table, norm_ffn1, ffn1_w_gate, ffn1_w_up, ffn1_w_down, norm_mix, norm_ffn2, ffn2_w_gate, ffn2_w_up, ffn2_w_down, ab_w_in, gm_norm_g, gm_norm_b, gm_w_s, gm_b_s, s5_lambda_re, s5_lambda_im, s5_log_dt, s5_b_re, s5_b_im, s5_c_re, s5_c_im, s5_d, s5_w_glu, s5_b_glu, ab_w_out, mla_w_a, mla_g_cq, mla_g_ckv, mla_w_uq, mla_g_qn, mla_g_qr, mla_g_kr, mla_w_uk, mla_g_kn, mla_w_uv, mla_w_o):
    b, l, d = x_prompt.shape
    nb, ds, _ = x_sample.shape
    depth = norm_ffn1.shape[0]
    past = page_table.shape[1] * cache_ckv.shape[2]
    x = (x_prompt.reshape(b * l, d), x_sample.reshape(nb * ds, d))
    cos, sin = _rope_tables(b, l, nb, ds, past)
    cache_krope_t = jnp.swapaxes(cache_krope, 2, 3)
    ckv_p, kr_p, ckv_s, kr_s = [], [], [], []
    s5r_p, s5i_p, s5r_s, s5i_s, gmv_s = [], [], [], [], []
    for layer in range(depth):
        x = _ffn(x, norm_ffn1[layer], ffn1_w_gate[layer].astype(BF16), ffn1_w_up[layer].astype(BF16),
                 ffn1_w_down[layer].astype(BF16))
        if layer % 2 == 0:
            i = layer // 2
            x, v_s, hr_p, hi_p, hr_s, hi_s = _even_layer(
                x, b, l, nb, ds, state_s5_re[i], state_s5_im[i], norm_mix[layer], ab_w_in[i],
                gm_norm_g[i], gm_norm_b[i], gm_w_s[i], gm_b_s[i], s5_lambda_re[i], s5_lambda_im[i],
                s5_log_dt[i], s5_b_re[i], s5_b_im[i], s5_c_re[i], s5_c_im[i], s5_d[i], s5_w_glu[i],
                s5_b_glu[i], ab_w_out[i])
            s5r_p.append(hr_p)
            s5i_p.append(hi_p)
            s5r_s.append(hr_s)
            s5i_s.append(hi_s)
            gmv_s.append(v_s)
        else:
            j = layer // 2
            x, c_p, k_p, c_s, k_s = _mla_layer(
                x, b, l, nb, ds, j, cos, sin, norm_mix[layer], cache_ckv, cache_krope_t, page_table,
                mla_w_a[j], mla_g_cq[j], mla_g_ckv[j], mla_w_uq[j], mla_g_qn[j], mla_g_qr[j],
                mla_g_kr[j], mla_w_uk[j], mla_g_kn[j], mla_w_uv[j], mla_w_o[j])
            ckv_p.append(c_p)
            kr_p.append(k_p)
            ckv_s.append(c_s)
            kr_s.append(k_s)
        x = _ffn(x, norm_ffn2[layer], ffn2_w_gate[layer].astype(BF16), ffn2_w_up[layer].astype(BF16),
                 ffn2_w_down[layer].astype(BF16), split_rows=b * l if layer == depth - 1 else None)
    xp = x[0].reshape(b, l, d)
    xs = x[1].reshape(nb, ds, d)
    return (xp, xs,
            jnp.stack(ckv_p), jnp.stack(kr_p), jnp.stack(ckv_s), jnp.stack(kr_s),
            jnp.stack(s5r_p), jnp.stack(s5i_p), jnp.stack(s5r_s), jnp.stack(s5i_s),
            jnp.stack(gmv_s))
```

```python
import functools
import math

import jax
import jax.numpy as jnp
import numpy as np
from jax import lax
from jax.experimental import pallas as pl
from jax.experimental.pallas import tpu as pltpu

F32 = jnp.float32
BF16 = jnp.bfloat16

EPS = 1e-6
GM_HEADS = 4
CHUNK = 128
S5_CH = 16
S5_STATE = 64
S5_QCHUNKS = 4
MLA_HEADS = 8
Q_LORA = 384
KV_LORA = 256
NOPE_DIM = 128
ROPE_DIM = 64
ROPE_HALF = ROPE_DIM // 2
V_DIM = 128
ROPE_BASE = 10000.0
ATTN_SCALE = (NOPE_DIM + ROPE_DIM) ** -0.5
Q_SCALE = ATTN_SCALE * math.log2(math.e)
HEAD_PAD = 256
LANES = 128
SUBLANES = 8
VMEM_LIMIT = 56 * 1024 * 1024


def _pick(n, prefs):
    for p in prefs:
        if n % p == 0:
            return p
    raise ValueError(f"no tile in {prefs} divides {n}")


def _params(*sem):
    return pltpu.CompilerParams(dimension_semantics=sem, vmem_limit_bytes=VMEM_LIMIT)


def _const_spec(shape):
    n = len(shape)
    return pl.BlockSpec(shape, lambda *_: (0,) * n, pipeline_mode=pl.Buffered(1))


def _rms(x, g):
    return x * lax.rsqrt(jnp.mean(x * x, axis=-1, keepdims=True) + EPS) * g


def _sigmoid(x):
    return 1.0 / (1.0 + jnp.exp(-x))


def _gelu_tanh(x):
    return 0.5 * x * (1.0 + jnp.tanh(math.sqrt(2.0 / math.pi) * (x + 0.044715 * (x * x * x))))


def _dot(a, b):
    return jnp.dot(a, b, preferred_element_type=F32)


def _lane_tile(x, n):
    return jnp.concatenate([x] * n, axis=1)


def _dot_nt(a, b):
    return lax.dot_general(a, b, (((1,), (1,)), ((), ())), preferred_element_type=F32)


def _ffn_body(*refs, fchunk, n_in, n_stage, has_attn, n_out, n_head_tiles):
    x_refs = refs[:n_in]
    k = n_in
    if has_attn:
        op_ref, os_ref, wuv_ref, wo_ref = refs[k:k + 4]
        k += 4
    stages = [refs[k + 4 * j:k + 4 * j + 4] for j in range(n_stage)]
    k += 4 * n_stage
    o_refs = refs[k:k + n_out]
    a_scr = refs[k + n_out]
    i = pl.program_id(0)
    if n_in == 1:
        x = x_refs[0][...]
    else:
        x = jnp.where(i < n_head_tiles, x_refs[0][...], x_refs[1][...])

    if has_attn:
        v_scr = refs[k + n_out + 1]

        def up_project(o_ref):
            for h in range(MLA_HEADS):
                v_scr[:, h * V_DIM:(h + 1) * V_DIM] = _dot(
                    o_ref[:, h * KV_LORA:(h + 1) * KV_LORA].astype(BF16), wuv_ref[h]).astype(BF16)

        @pl.when(i < n_head_tiles)
        def _():
            up_project(op_ref)

        @pl.when(i >= n_head_tiles)
        def _():
            up_project(os_ref)

        x = x + _dot(v_scr[...], wo_ref[...])

    for g_ref, wg_ref, wu_ref, wd_ref in stages:
        h = _rms(x, g_ref[...]).astype(BF16)
        d_ff = wg_ref.shape[1]
        for c in range(0, d_ff, fchunk):
            g = _dot(h, wg_ref[:, c:c + fchunk])
            u = _dot(h, wu_ref[:, c:c + fchunk])
            a_scr[:, c:c + fchunk] = (g * _sigmoid(g) * u).astype(BF16)
        x = x + 0.5 * _dot(a_scr[...], wd_ref[...])

    if n_out == 1:
        o_refs[0][...] = x
    else:
        @pl.when(i < n_head_tiles)
        def _():
            o_refs[0][...] = x

        @pl.when(i >= n_head_tiles)
        def _():
            o_refs[1][...] = x


def _ffn(xs, stages, attn=None, split_rows=None):
    xs = xs if isinstance(xs, tuple) else (xs,)
    d = xs[0].shape[1]
    t = sum(x.shape[0] for x in xs)
    d_ff = stages[0][1].shape[1]
    if len(xs) == 2:
        head = xs[0].shape[0]
    elif attn is not None:
        head = attn[0].shape[0]
    else:
        head = split_rows or t
    tm = _pick(math.gcd(head, t - head) if head < t else t, (512, 256, 128, 64, 32, 16, 8))
    n_head = head // tm
    head_map = lambda i: (jnp.minimum(i, n_head - 1), 0)
    tail_map = lambda i: (jnp.maximum(i - n_head, 0), 0)
    row_map = lambda i: (i, 0)
    args = list(xs)
    in_specs = ([pl.BlockSpec((tm, d), row_map)] if len(xs) == 1 else
                [pl.BlockSpec((tm, d), head_map), pl.BlockSpec((tm, d), tail_map)])
    scratch = [pltpu.VMEM((tm, d_ff), BF16)]
    if attn is not None:
        o_head, o_tail, wuv, wo = attn
        args += [o_head, o_tail, wuv, wo]
        in_specs += [pl.BlockSpec((tm, o_head.shape[1]), head_map),
                     pl.BlockSpec((tm, o_tail.shape[1]), tail_map),
                     _const_spec(wuv.shape), _const_spec(wo.shape)]
        scratch.append(pltpu.VMEM((tm, MLA_HEADS * V_DIM), BF16))
    for g, wg, wu, wd in stages:
        args += [g.reshape(1, d), wg, wu, wd]
        in_specs += [_const_spec((1, d)), _const_spec(wg.shape), _const_spec(wu.shape), _const_spec(wd.shape)]
    if split_rows is None:
        out_shape = jax.ShapeDtypeStruct((t, d), F32)
        out_specs = pl.BlockSpec((tm, d), row_map)
    else:
        out_shape = (jax.ShapeDtypeStruct((head, d), F32), jax.ShapeDtypeStruct((t - head, d), F32))
        out_specs = (pl.BlockSpec((tm, d), head_map), pl.BlockSpec((tm, d), tail_map))
    return pl.pallas_call(
        functools.partial(_ffn_body, fchunk=256, n_in=len(xs), n_stage=len(stages), has_attn=attn is not None,
                          n_out=1 if split_rows is None else 2, n_head_tiles=n_head),
        out_shape=out_shape,
        grid=(t // tm,),
        in_specs=in_specs,
        out_specs=out_specs,
        scratch_shapes=scratch,
        compiler_params=_params("arbitrary"),
        name="ffn",
    )(*args)


def _s5_tables(lam_re, lam_im, log_dt, b_re, b_im, c_re, c_im):
    lr, li = lam_re.astype(F32), lam_im.astype(F32)
    dt = jnp.exp(log_dt.astype(F32))[:, None]
    mag = jnp.exp(lr * dt)
    ar, ai = mag * jnp.cos(li * dt), mag * jnp.sin(li * dt)
    er, ei = ar - 1.0, ai
    den = lr * lr + li * li
    qr = (er * lr + ei * li) / den
    qi = (ei * lr - er * li) / den
    br, bi = b_re.astype(F32), b_im.astype(F32)
    bbr = qr[..., None] * br - qi[..., None] * bi
    bbi = qr[..., None] * bi + qi[..., None] * br
    g = lr.shape[0]
    gq = g // S5_QCHUNKS
    eye = jnp.eye(gq, dtype=F32)

    def b_blk(bb):
        bq = bb.reshape(S5_QCHUNKS, gq, S5_STATE, S5_CH)
        return jnp.einsum('qgnc,gh->qhcgn', bq, eye).reshape(S5_QCHUNKS, gq * S5_CH, gq * S5_STATE)

    def c_blk(cc):
        cq = cc.astype(F32).reshape(S5_QCHUNKS, gq, S5_CH, S5_STATE)
        return jnp.einsum('qgcn,gh->qgnhc', cq, eye).reshape(S5_QCHUNKS, gq * S5_STATE, gq * S5_CH)

    bblk = jnp.concatenate([b_blk(bbr), b_blk(bbi)], axis=2).astype(BF16)
    cblk = jnp.concatenate([c_blk(c_re), -c_blk(c_im)], axis=1).astype(BF16)
    a_tab = jnp.stack([ar.reshape(S5_QCHUNKS, -1), ai.reshape(S5_QCHUNKS, -1)], axis=1)
    return a_tab, bblk, cblk


def _pack_state(hr, hi):
    n = hr.shape[0]
    r = hr.astype(F32).reshape(n, S5_QCHUNKS, 1, -1)
    i = hi.astype(F32).reshape(n, S5_QCHUNKS, 1, -1)
    return jnp.concatenate([r, i], axis=2).reshape(n, -1)


def _unpack_state(st, groups):
    n = st.shape[0]
    s4 = st.reshape(n, S5_QCHUNKS, 2, -1)
    return (s4[:, :, 0].reshape(n, groups, S5_STATE), s4[:, :, 1].reshape(n, groups, S5_STATE))


def _cmul(ar, ai, br, bi):
    return ar * br - ai * bi, ar * bi + ai * br


def _even_prompt_body(x_ref, gmix_ref, win_ref, lng_ref, lnb_ref, wtril_ref, sbias_ref,
                      atab_ref, bblk_ref, cblk_ref, d_ref, wglu_ref, bglu_ref, wout_ref,
                      o_ref, st_ref,
                      us_scr, usp_scr, xs_scr, yp_scr, y_scr, cat_scr, carry_scr, *, tt):
    m = tt // SUBLANES
    gw = lng_ref.shape[1]
    sw = d_ref.shape[1]
    hs = sw
    t_idx = pl.program_id(1)

    @pl.when(t_idx == 0)
    def _():
        carry_scr[...] = jnp.zeros_like(carry_scr)

    x = x_ref[...]
    hn = _rms(x, gmix_ref[...]).astype(BF16)
    z = _dot(hn, win_ref[...])
    u = _gelu_tanh(z[:, :gw])
    v = _gelu_tanh(z[:, gw:2 * gw])
    mu = jnp.mean(v, axis=-1, keepdims=True)
    vc = v - mu
    v = vc * lax.rsqrt(jnp.mean(vc * vc, axis=-1, keepdims=True) + EPS) * lng_ref[...] + lnb_ref[...]
    vb = v.astype(BF16)
    hd = gw // GM_HEADS
    for c in range(tt // CHUNK):
        rows = slice(c * CHUNK, (c + 1) * CHUNK)
        for h in range(GM_HEADS):
            cols = slice(h * hd, (h + 1) * hd)
            s = _dot(wtril_ref[h], vb[rows, cols]) + sbias_ref[:, cols]
            cat_scr[rows, cols] = (u[rows, cols] * s).astype(BF16)

    qw = sw // S5_QCHUNKS
    for q in range(S5_QCHUNKS):
        us_scr[q] = z[:, 2 * gw + q * qw:2 * gw + (q + 1) * qw]
    sub = lax.broadcasted_iota(jnp.int32, (SUBLANES, hs), 0)

    def shift_rows(val, k):
        return jnp.where(sub >= k, pltpu.roll(val, k, axis=0), 0.0)

    for q in range(S5_QCHUNKS):
        for i in range(m):
            usp_scr[i * SUBLANES:(i + 1) * SUBLANES, :] = us_scr.at[q][pl.ds(i, SUBLANES, stride=m), :]
        xs_scr[...] = _dot(usp_scr[...].astype(BF16), bblk_ref[q])
        ar = atab_ref[q, 0:1, :]
        ai = atab_ref[q, 1:2, :]
        arb = jnp.broadcast_to(ar, (SUBLANES, hs))
        aib = jnp.broadcast_to(ai, (SUBLANES, hs))

        def step(i, carry, store):
            hr, hi = carry
            r0 = pl.multiple_of(i * SUBLANES, SUBLANES)
            xr = xs_scr[pl.ds(r0, SUBLANES), 0:hs]
            xi = xs_scr[pl.ds(r0, SUBLANES), hs:2 * hs]
            nr = arb * hr - aib * hi + xr
            ni = arb * hi + aib * hr + xi
            if store:
                xs_scr[pl.ds(r0, SUBLANES), 0:hs] = nr
                xs_scr[pl.ds(r0, SUBLANES), hs:2 * hs] = ni
            return nr, ni

        zero = jnp.zeros((SUBLANES, hs), F32)
        er, ei = lax.fori_loop(0, m, functools.partial(step, store=False), (zero, zero), unroll=True)

        pr, pi = ar, ai
        for _ in range(int(math.log2(m))):
            pr, pi = _cmul(pr, pi, pr, pi)
        a1 = (pr, pi)
        a2 = _cmul(*a1, *a1)
        a4 = _cmul(*a2, *a2)
        fr, fi = er, ei
        for k, (kr, ki) in ((1, a1), (2, a2), (4, a4)):
            sr, si = shift_rows(fr, k), shift_rows(fi, k)
            dr, di = _cmul(kr, ki, sr, si)
            fr, fi = fr + dr, fi + di
        wr, wi = jnp.ones((SUBLANES, hs), F32), jnp.zeros((SUBLANES, hs), F32)
        for bit, (kr, ki) in ((1, a1), (2, a2), (4, a4)):
            sel = (sub & bit) != 0
            wr, wi = _cmul(wr, wi, jnp.where(sel, kr, 1.0), jnp.where(sel, ki, 0.0))
        cr = carry_scr[q, SUBLANES - 1:SUBLANES, 0:hs]
        ci = carry_scr[q, SUBLANES - 1:SUBLANES, hs:2 * hs]
        tr, ti = _cmul(wr, wi, cr, ci)
        h0r = tr + shift_rows(fr, 1)
        h0i = ti + shift_rows(fi, 1)
        nr, ni = _cmul(a1[0], a1[1], h0r, h0i)
        carry_scr[q, :, 0:hs] = nr + er
        carry_scr[q, :, hs:2 * hs] = ni + ei
        st_ref[0, :, q * 2 * hs:(q + 1) * 2 * hs] = carry_scr[q, SUBLANES - 1:SUBLANES, :]

        lax.fori_loop(0, m, functools.partial(step, store=True), (h0r, h0i), unroll=True)
        yp_scr[...] = _dot(xs_scr[...].astype(BF16), cblk_ref[q])
        for s in range(SUBLANES):
            y_scr[s * m:(s + 1) * m, q * qw:(q + 1) * qw] = (
                yp_scr[pl.ds(s, m, stride=SUBLANES), :]
                + d_ref[:, q * qw:(q + 1) * qw] * us_scr[q, s * m:(s + 1) * m, :])

    g = _gelu_tanh(y_scr[...])
    gate = _sigmoid(_dot(g.astype(BF16), wglu_ref[...]) + bglu_ref[...])
    cat_scr[:, gw:gw + sw] = (g * gate).astype(BF16)
    o_ref[...] = x + _dot(cat_scr[...], wout_ref[...])


def _even_prompt(x, b, l, gmix, win, lng, lnb, wtril, sbias, atab, bblk, cblk, dvec, wglu, bglu, wout):
    d = x.shape[1]
    gw, sw = lng.shape[1], dvec.shape[1]
    tt = _pick(l, (256, 128))
    nt = l // tt
    hs2 = 2 * sw
    args = (x, gmix, win, lng, lnb, wtril, sbias, atab, bblk, cblk, dvec, wglu, bglu, wout)
    in_specs = [pl.BlockSpec((tt, d), lambda bi, ti: (bi * nt + ti, 0))]
    in_specs += [_const_spec(a.shape) for a in args[1:]]
    return pl.pallas_call(
        functools.partial(_even_prompt_body, tt=tt),
        out_shape=(jax.ShapeDtypeStruct((b * l, d), F32),
                   jax.ShapeDtypeStruct((b, 1, S5_QCHUNKS * hs2), F32)),
        grid=(b, nt),
        in_specs=in_specs,
        out_specs=(pl.BlockSpec((tt, d), lambda bi, ti: (bi * nt + ti, 0)),
                   pl.BlockSpec((1, 1, S5_QCHUNKS * hs2), lambda bi, ti: (bi, 0, 0))),
        scratch_shapes=[
            pltpu.VMEM((S5_QCHUNKS, tt, sw // S5_QCHUNKS), F32),
            pltpu.VMEM((tt, sw // S5_QCHUNKS), F32),
            pltpu.VMEM((tt, hs2), F32),
            pltpu.VMEM((tt, sw // S5_QCHUNKS), F32),
            pltpu.VMEM((tt, sw), F32),
            pltpu.VMEM((tt, gw + sw), BF16),
            pltpu.VMEM((S5_QCHUNKS, SUBLANES, hs2), F32),
        ],
        compiler_params=_params("arbitrary", "arbitrary"),
        name="even_prompt",
    )(*args)


def _even_sample_body(x_ref, h0_ref, gmix_ref, win_ref, lng_ref, lnb_ref, scoef_ref, sbias_ref,
                      atab_ref, bblk_ref, cblk_ref, d_ref, wglu_ref, bglu_ref, wout_ref,
                      o_ref, v_ref, st_ref,
                      xs_scr, y_scr, cat_scr, *, ds, nb):
    gw = lng_ref.shape[1]
    sw = d_ref.shape[1]
    hs = sw
    x = x_ref[...]
    hn = _rms(x, gmix_ref[...]).astype(BF16)
    z = _dot(hn, win_ref[...])
    u = _gelu_tanh(z[:, :gw])
    v = _gelu_tanh(z[:, gw:2 * gw])
    mu = jnp.mean(v, axis=-1, keepdims=True)
    vc = v - mu
    v = vc * lax.rsqrt(jnp.mean(vc * vc, axis=-1, keepdims=True) + EPS) * lng_ref[...] + lnb_ref[...]
    v_ref[...] = v
    for t in range(ds):
        s = jnp.broadcast_to(sbias_ref[t:t + 1, :], (nb, gw))
        for sp in range(t + 1):
            s = s + scoef_ref[t * ds + sp:t * ds + sp + 1, :] * v[sp * nb:(sp + 1) * nb, :]
        cat_scr[t * nb:(t + 1) * nb, 0:gw] = (u[t * nb:(t + 1) * nb, :] * s).astype(BF16)

    us = z[:, 2 * gw:]
    usb = us.astype(BF16)
    qw = sw // S5_QCHUNKS
    for q in range(S5_QCHUNKS):
        xs_scr[...] = _dot(usb[:, q * qw:(q + 1) * qw], bblk_ref[q])
        ar = atab_ref[q, 0:1, :]
        ai = atab_ref[q, 1:2, :]
        hr = h0_ref[:, q * 2 * hs:q * 2 * hs + hs]
        hi = h0_ref[:, q * 2 * hs + hs:(q + 1) * 2 * hs]
        for t in range(ds):
            rows = slice(t * nb, (t + 1) * nb)
            nr = ar * hr - ai * hi + xs_scr[rows, 0:hs]
            ni = ar * hi + ai * hr + xs_scr[rows, hs:2 * hs]
            xs_scr[rows, 0:hs] = nr
            xs_scr[rows, hs:2 * hs] = ni
            hr, hi = nr, ni
        st_ref[:, q * 2 * hs:q * 2 * hs + hs] = hr
        st_ref[:, q * 2 * hs + hs:(q + 1) * 2 * hs] = hi
        y_scr[:, q * qw:(q + 1) * qw] = _dot(xs_scr[...].astype(BF16), cblk_ref[q])
    g = _gelu_tanh(y_scr[...] + d_ref[...] * us)
    gate = _sigmoid(_dot(g.astype(BF16), wglu_ref[...]) + bglu_ref[...])
    cat_scr[:, gw:gw + sw] = (g * gate).astype(BF16)
    o_ref[...] = x + _dot(cat_scr[...], wout_ref[...])


def _even_sample(xs_tm, h0, gmix, win, lng, lnb, scoef, sbias8, atab, bblk, cblk, dvec, wglu, bglu, wout, *, ds):
    t, d = xs_tm.shape
    nb = t // ds
    gw, sw = lng.shape[1], dvec.shape[1]
    args = (xs_tm, h0, gmix, win, lng, lnb, scoef, sbias8, atab, bblk, cblk, dvec, wglu, bglu, wout)
    return pl.pallas_call(
        functools.partial(_even_sample_body, ds=ds, nb=nb),
        out_shape=(jax.ShapeDtypeStruct((t, d), F32),
                   jax.ShapeDtypeStruct((t, gw), F32),
                   jax.ShapeDtypeStruct(h0.shape, F32)),
        grid=(1,),
        in_specs=[_const_spec(a.shape) for a in args],
        out_specs=tuple(pl.BlockSpec(s, lambda *_: (0, 0)) for s in ((t, d), (t, gw), h0.shape)),
        scratch_shapes=[
            pltpu.VMEM((t, 2 * sw), F32),
            pltpu.VMEM((t, sw), F32),
            pltpu.VMEM((t, gw + sw), BF16),
        ],
        compiler_params=_params("arbitrary"),
        name="even_sample",
    )(*args)


def _mla_proj_body(x_ref, cos_ref, sin_ref, gmix_ref, wa_ref, gcq_ref, gckv_ref, gkr_ref,
                   wuq_ref, gqn_ref, gq1_ref, gq2_ref, wuk_ref, gkn_ref, ones_ref, perm_ref,
                   ckv_ref, kr_ref, cbf_ref, qcat_ref, kcat_ref):
    tm = x_ref.shape[0]
    x = x_ref[...]
    hn = _rms(x, gmix_ref[...]).astype(BF16)
    a = _dot(hn, wa_ref[...])
    cq = _rms(a[:, :Q_LORA], gcq_ref[...])
    ckv = _rms(a[:, Q_LORA:Q_LORA + KV_LORA], gckv_ref[...])
    ckv_ref[...] = ckv
    ckv_b = ckv.astype(BF16)
    cbf_ref[...] = ckv_b

    cos = cos_ref[...]
    sin = sin_ref[...]
    slab = a[:, Q_LORA + KV_LORA:]
    ms = jnp.sum(slab * slab, axis=-1, keepdims=True) * (1.0 / ROPE_DIM)
    kn = slab * lax.rsqrt(ms + EPS) * gkr_ref[...]
    lane = lax.broadcasted_iota(jnp.int32, (tm, LANES), 1)
    rot = jnp.where(lane < ROPE_HALF, -pltpu.roll(kn, LANES - ROPE_HALF, axis=1),
                    pltpu.roll(kn, ROPE_HALF, axis=1))
    kr = kn * cos + rot * sin
    kr_ref[...] = kr[:, :ROPE_DIM]
    kr_pad = jnp.where(lane < ROPE_DIM, kr, 0.0).astype(BF16)

    q = _dot(cq.astype(BF16), wuq_ref[...])
    k = _dot(ckv_b, wuk_ref[...])
    nn = MLA_HEADS * NOPE_DIM
    rw = MLA_HEADS * ROPE_HALF
    x1 = q[:, nn:nn + rw]
    x2 = q[:, nn + rw:nn + 2 * rw]
    ssq = x1 * x1 + x2 * x2
    ssq_hi = ssq.astype(BF16)
    ssq_lo = (ssq - ssq_hi.astype(F32)).astype(BF16)
    gsum = _dot(ssq_hi, ones_ref[...]) + _dot(ssq_lo, ones_ref[...])
    r = lax.rsqrt(gsum * (1.0 / ROPE_DIM) + EPS)
    x1n = x1 * r * gq1_ref[...]
    x2n = x2 * r * gq2_ref[...]
    reps = rw // LANES
    cos2 = jnp.concatenate([cos] * reps, axis=1)
    sin2 = jnp.concatenate([sin] * reps, axis=1)
    o1 = (x1n * cos2 - x2n * sin2) * Q_SCALE
    o2 = (x2n * cos2 + x1n * sin2) * Q_SCALE
    qr_cat = _dot(jnp.concatenate([o1, o2], axis=1).astype(BF16), perm_ref[...]).astype(BF16)
    for h in range(MLA_HEADS):
        nsl = slice(h * NOPE_DIM, (h + 1) * NOPE_DIM)
        qcat_ref[:, h * HEAD_PAD:h * HEAD_PAD + NOPE_DIM] = (_rms(q[:, nsl], gqn_ref[...]) * Q_SCALE).astype(BF16)
        qcat_ref[:, h * HEAD_PAD + NOPE_DIM:(h + 1) * HEAD_PAD] = qr_cat[:, h * LANES:(h + 1) * LANES]
        kcat_ref[:, h * HEAD_PAD:h * HEAD_PAD + NOPE_DIM] = _rms(k[:, nsl], gkn_ref[...]).astype(BF16)
        kcat_ref[:, h * HEAD_PAD + NOPE_DIM:(h + 1) * HEAD_PAD] = kr_pad


def _mla_proj(x, cos, sin, consts):
    t, d = x.shape
    tm = _pick(t, (256, 128, 64, 8))
    hw = MLA_HEADS * HEAD_PAD
    row = lambda w: pl.BlockSpec((tm, w), lambda i: (i, 0))
    return pl.pallas_call(
        _mla_proj_body,
        out_shape=(jax.ShapeDtypeStruct((t, KV_LORA), F32),
                   jax.ShapeDtypeStruct((t, ROPE_DIM), F32),
                   jax.ShapeDtypeStruct((t, KV_LORA), BF16),
                   jax.ShapeDtypeStruct((t, hw), BF16),
                   jax.ShapeDtypeStruct((t, hw), BF16)),
        grid=(t // tm,),
        in_specs=[row(d), row(LANES), row(LANES)] + [_const_spec(c.shape) for c in consts],
        out_specs=(row(KV_LORA), row(ROPE_DIM), row(KV_LORA), row(hw), row(hw)),
        compiler_params=_params("arbitrary"),
        name="mla_proj",
    )(x, cos, sin, *consts)


def _attn_prompt_body(qi_ref, kj_ref, q_ref, k_ref, c_ref, o_ref, m_scr, l_scr, acc_scr, *, tq):
    p = pl.program_id(1)
    qi = qi_ref[p]
    kj = kj_ref[p]

    @pl.when(kj == 0)
    def _():
        m_scr[...] = jnp.full_like(m_scr, -jnp.inf)
        l_scr[...] = jnp.zeros_like(l_scr)
        acc_scr[...] = jnp.zeros_like(acc_scr)

    def tile(sub, masked):
        ksl = slice(sub * tq, (sub + 1) * tq)
        c = c_ref[ksl, :]
        nrep = tq // LANES
        vrep = KV_LORA // LANES
        if masked:
            row = lax.broadcasted_iota(jnp.int32, (tq, tq), 0)
            col = lax.broadcasted_iota(jnp.int32, (tq, tq), 1)
            keep = row >= col
        for h in range(MLA_HEADS):
            hsl = slice(h * HEAD_PAD, (h + 1) * HEAD_PAD)
            s = _dot_nt(q_ref[:, hsl], k_ref[ksl, hsl])
            if masked:
                s = jnp.where(keep, s, -jnp.inf)
            m_prev = m_scr[h]
            m_new = jnp.maximum(m_prev, jnp.max(s, axis=-1, keepdims=True))
            alpha = jnp.exp2(m_prev - m_new)
            pexp = jnp.exp2(s - _lane_tile(m_new, nrep))
            psum = pexp[:, 0:LANES]
            for j in range(1, nrep):
                psum = psum + pexp[:, j * LANES:(j + 1) * LANES]
            l_scr[h] = alpha * l_scr[h] + psum
            acc_scr[h] = _lane_tile(alpha, vrep) * acc_scr[h] + _dot(pexp.astype(BF16), c)
            m_scr[h] = m_new

    def finalize():
        for h in range(MLA_HEADS):
            o_ref[:, h * KV_LORA:(h + 1) * KV_LORA] = (
                acc_scr[h] / jnp.sum(l_scr[h], axis=-1, keepdims=True)).astype(BF16)

    @pl.when(2 * kj + 1 < qi)
    def _():
        tile(0, False)
        tile(1, False)

    @pl.when(2 * kj + 1 == qi)
    def _():
        tile(0, False)
        tile(1, True)
        finalize()

    @pl.when(2 * kj == qi)
    def _():
        tile(0, True)
        finalize()


def _attn_prompt(qcat, kcat, cbf, b, l):
    hw = qcat.shape[1]
    tq = _pick(l, (512, 256, 128))
    nq = l // tq
    nkb = nq // 2
    pairs = [(i, j) for i in range(nq) for j in range(i // 2 + 1)]
    qi_tab = jnp.asarray([i for i, _ in pairs], jnp.int32)
    ki_tab = jnp.asarray([j for _, j in pairs], jnp.int32)
    grid_spec = pltpu.PrefetchScalarGridSpec(
        num_scalar_prefetch=2,
        grid=(b, len(pairs)),
        in_specs=[
            pl.BlockSpec((tq, hw), lambda bi, p, qi, kj: (bi * nq + qi[p], 0)),
            pl.BlockSpec((2 * tq, hw), lambda bi, p, qi, kj: (bi * nkb + kj[p], 0)),
            pl.BlockSpec((2 * tq, KV_LORA), lambda bi, p, qi, kj: (bi * nkb + kj[p], 0)),
        ],
        out_specs=pl.BlockSpec((tq, MLA_HEADS * KV_LORA), lambda bi, p, qi, kj: (bi * nq + qi[p], 0)),
        scratch_shapes=[
            pltpu.VMEM((MLA_HEADS, tq, LANES), F32),
            pltpu.VMEM((MLA_HEADS, tq, LANES), F32),
            pltpu.VMEM((MLA_HEADS, tq, KV_LORA), F32),
        ],
    )
    return pl.pallas_call(
        functools.partial(_attn_prompt_body, tq=tq),
        out_shape=jax.ShapeDtypeStruct((b * l, MLA_HEADS * KV_LORA), BF16),
        grid_spec=grid_spec,
        compiler_params=_params("arbitrary", "arbitrary"),
        name="attn_prompt",
    )(qi_tab, ki_tab, qcat, kcat, cbf)


def _attn_sample_body(pt_ref, q_ref, cn_ref, krn_ref, wukt_ref, gkn_ref, ckv_hbm, krc_hbm,
                      o_ref, cbuf, krbuf, sems, lhs_scr, big_scr, tail_c, tail_kr,
                      *, layer, n_pages, page, tk, ds):
    b = pl.program_id(0)
    nb = pl.num_programs(0)
    slot = lax.rem(b, 2)
    nh = MLA_HEADS
    nrow = nh * ds
    kw = nh * NOPE_DIM
    ppt = tk // page
    nt = n_pages // ppt

    def page_copies(seq, sl, j):
        pg = pt_ref[seq, j]
        r0 = pl.multiple_of(j * page, page)
        return (pltpu.make_async_copy(ckv_hbm.at[layer, pg], cbuf.at[sl, pl.ds(r0, page), :], sems.at[0, sl]),
                pltpu.make_async_copy(krc_hbm.at[layer, pg], krbuf.at[sl, j], sems.at[1, sl]))

    def start_fetch(seq, sl):
        def body(j, carry):
            for cp in page_copies(seq, sl, j):
                cp.start()
            return carry
        lax.fori_loop(0, n_pages, body, 0, unroll=8)

    @pl.when(b == 0)
    def _():
        start_fetch(0, 0)
        lhs_scr[0:kw, :] = wukt_ref[...]
        tail_c[...] = jnp.zeros_like(tail_c)
        tail_kr[...] = jnp.zeros_like(tail_kr)

    for j in range(n_pages):
        for cp in page_copies(b, slot, j):
            cp.wait()

    @pl.when(b + 1 < nb)
    def _():
        start_fetch(b + 1, 1 - slot)

    qf = q_ref[0]
    qabs, qrope = [], []
    for h in range(nh):
        qn = (qf[:, h * HEAD_PAD:h * HEAD_PAD + NOPE_DIM] * gkn_ref[...]).astype(BF16)
        qabs.append(_dot(qn, wukt_ref[h * NOPE_DIM:(h + 1) * NOPE_DIM, :]))
        qrope.append(qf[:, h * HEAD_PAD + NOPE_DIM:h * HEAD_PAD + NOPE_DIM + ROPE_DIM])
    lhs_scr[kw:kw + nrow, :] = jnp.concatenate(qabs, axis=0).astype(BF16)
    qr_b = jnp.concatenate(qrope, axis=0).astype(BF16)
    tail_c[0:ds, :] = cn_ref[0]
    tail_kr[:, 0:ds] = krn_ref[0]

    def softmax_update(k_rows, sraw, cb, krt, carry, width, masked):
        m_prev, l_prev, acc = carry
        rinv = []
        for h in range(nh):
            kt = k_rows(h)
            ss = jnp.sum(kt * kt, axis=0, keepdims=True) * (1.0 / NOPE_DIM)
            rinv.append(jnp.broadcast_to(lax.rsqrt(ss + EPS), (ds, width)))
        rinv = jnp.concatenate(rinv, axis=0)
        s = sraw * rinv + _dot(qr_b, krt.astype(BF16))
        if masked:
            qpos = lax.rem(lax.broadcasted_iota(jnp.int32, (nrow, width), 0), ds)
            kpos = lax.broadcasted_iota(jnp.int32, (nrow, width), 1)
            s = jnp.where(kpos <= qpos, s, -jnp.inf)
        m_new = jnp.maximum(m_prev, jnp.max(s, axis=-1, keepdims=True))
        alpha = jnp.exp2(m_prev - m_new)
        pexp = jnp.exp2(s - m_new)
        l_new = alpha * l_prev + jnp.sum(pexp, axis=-1, keepdims=True)
        acc = alpha * acc + _dot(pexp.astype(BF16), cb)
        return m_new, l_new, acc

    def c_tile(t):
        return cbuf[slot, t * tk:(t + 1) * tk, :].astype(BF16)

    def scores_into(t, sb):
        big_scr[sb] = _dot_nt(lhs_scr[...], c_tile(t))

    def consume(t, sb, carry):
        krt = jnp.concatenate([krbuf[slot, t * ppt + d] for d in range(ppt)], axis=1)
        return softmax_update(lambda h: big_scr[sb, h * NOPE_DIM:(h + 1) * NOPE_DIM, :],
                              big_scr[sb, kw:kw + nrow, :], c_tile(t), krt, carry, tk, False)

    carry = (jnp.full((nrow, 1), -jnp.inf, F32), jnp.zeros((nrow, 1), F32),
             jnp.zeros((nrow, KV_LORA), F32))
    scores_into(0, 0)
    for t in range(nt):
        if t + 1 < nt:
            scores_into(t + 1, (t + 1) % 2)
        carry = consume(t, t % 2, carry)
    tcb = tail_c[...].astype(BF16)
    tbig = _dot_nt(lhs_scr[...], tcb)
    _, l_fin, acc = softmax_update(lambda h: tbig[h * NOPE_DIM:(h + 1) * NOPE_DIM, :],
                                   tbig[kw:kw + nrow, :], tcb, tail_kr[...], carry,
                                   tail_c.shape[0], True)
    out = acc / l_fin
    for h in range(nh):
        o_ref[:, h * KV_LORA:(h + 1) * KV_LORA] = out[h * ds:(h + 1) * ds, :]


def _attn_sample(page_table, q_s, c_new, krt_new, wukt, gkn, cache_ckv, cache_krope_t, *, layer):
    nb, ds, hw = q_s.shape
    n_pages = page_table.shape[1]
    page = cache_ckv.shape[2]
    tk = _pick(n_pages * page, (512, 256, 128))
    nrow = MLA_HEADS * ds
    kw = MLA_HEADS * NOPE_DIM
    grid_spec = pltpu.PrefetchScalarGridSpec(
        num_scalar_prefetch=1,
        grid=(nb,),
        in_specs=[
            pl.BlockSpec((1, ds, hw), lambda i, pt: (i, 0, 0)),
            pl.BlockSpec((1, ds, KV_LORA), lambda i, pt: (i, 0, 0)),
            pl.BlockSpec((1, ROPE_DIM, ds), lambda i, pt: (i, 0, 0)),
            pl.BlockSpec((kw, KV_LORA), lambda i, pt: (0, 0)),
            pl.BlockSpec((1, NOPE_DIM), lambda i, pt: (0, 0)),
            pl.BlockSpec(memory_space=pl.ANY),
            pl.BlockSpec(memory_space=pl.ANY),
        ],
        out_specs=pl.BlockSpec((ds, MLA_HEADS * KV_LORA), lambda i, pt: (i, 0)),
        scratch_shapes=[
            pltpu.VMEM((2, n_pages * page, KV_LORA), F32),
            pltpu.VMEM((2, n_pages, ROPE_DIM, page), F32),
            pltpu.SemaphoreType.DMA((2, 2)),
            pltpu.VMEM((kw + nrow, KV_LORA), BF16),
            pltpu.VMEM((2, kw + nrow, tk), F32),
            pltpu.VMEM((LANES, KV_LORA), F32),
            pltpu.VMEM((ROPE_DIM, LANES), F32),
        ],
    )
    return pl.pallas_call(
        functools.partial(_attn_sample_body, layer=layer, n_pages=n_pages, page=page, tk=tk, ds=ds),
        out_shape=jax.ShapeDtypeStruct((nb * ds, MLA_HEADS * KV_LORA), F32),
        grid_spec=grid_spec,
        compiler_params=_params("arbitrary"),
        name="attn_sample",
    )(page_table, q_s, c_new, krt_new, wukt, gkn, cache_ckv, cache_krope_t)


def _even_layer(x, b, l, nb, ds, h0r, h0i, gmix, w_in, gm_g, gm_b, gm_w_s, gm_b_s,
                lam_re, lam_im, log_dt, b_re, b_im, c_re, c_im, dvec, w_glu, b_glu, w_out):
    d = x.shape[1]
    gw = gm_g.shape[0]
    sw = dvec.shape[0]
    groups = lam_re.shape[0]
    atab, bblk, cblk = _s5_tables(lam_re, lam_im, log_dt, b_re, b_im, c_re, c_im)
    causal = jnp.tril(jnp.ones((CHUNK, CHUNK), dtype=bool))
    w_tril = jnp.where(causal[None], gm_w_s, 0.0)
    sbias = jnp.repeat(gm_b_s.T, gw // GM_HEADS, axis=1)
    scoef = jnp.repeat(w_tril[:, :ds, :ds].transpose(1, 2, 0).reshape(ds * ds, GM_HEADS),
                       gw // GM_HEADS, axis=1)
    common = (gmix.reshape(1, d), w_in.astype(BF16), gm_g.reshape(1, gw), gm_b.reshape(1, gw))
    tail = (atab, bblk, cblk, dvec.reshape(1, sw), w_glu.astype(BF16), b_glu.reshape(1, sw),
            w_out.astype(BF16))
    yp, st_p = _even_prompt(x, b, l, *common, w_tril.astype(BF16), sbias, *tail)
    xs_tm = x[b * l:].reshape(nb, ds, d).transpose(1, 0, 2).reshape(ds * nb, d)
    ys_tm, v_tm, st_s = _even_sample(xs_tm, _pack_state(h0r, h0i), *common, scoef, sbias[:ds], *tail, ds=ds)
    ys = ys_tm.reshape(ds, nb, d).transpose(1, 0, 2).reshape(nb * ds, d)
    v_s = v_tm.reshape(ds, nb, gw).transpose(1, 0, 2)
    x_new = (yp, ys)
    hr_p, hi_p = _unpack_state(st_p.reshape(b, -1), groups)
    hr_s, hi_s = _unpack_state(st_s, groups)
    return x_new, v_s, hr_p, hi_p, hr_s, hi_s


def _rope_tables(b, l, nb, ds, past):
    inv = ROPE_BASE ** (-jnp.arange(ROPE_HALF, dtype=F32) * (2.0 / ROPE_DIM))
    pos_p = jnp.arange(l, dtype=jnp.int32)
    pos_s = past + jnp.arange(ds, dtype=jnp.int32)
    pos = jnp.concatenate([jnp.tile(pos_p, b), jnp.tile(pos_s, nb)])
    ang = pos.astype(F32)[:, None] * inv[None, :]
    reps = LANES // ROPE_HALF
    return jnp.tile(jnp.cos(ang), (1, reps)), jnp.tile(jnp.sin(ang), (1, reps))


def _mla_layer(x, b, l, nb, ds, layer, cos, sin, gmix, cache_ckv, cache_krope_t, page_table,
               w_a, g_cq, g_ckv, w_uq, g_qn, g_qr, g_kr, w_uk, g_kn, w_uv, w_o):
    d = x.shape[1]
    nh = MLA_HEADS
    wa_pad = jnp.concatenate([w_a, jnp.zeros((d, LANES - ROPE_DIM), w_a.dtype)], axis=1).astype(BF16)
    wq3 = w_uq.reshape(Q_LORA, nh, NOPE_DIM + ROPE_DIM)
    wuq = jnp.concatenate([wq3[:, :, :NOPE_DIM].reshape(Q_LORA, -1),
                           wq3[:, :, NOPE_DIM:NOPE_DIM + ROPE_HALF].reshape(Q_LORA, -1),
                           wq3[:, :, NOPE_DIM + ROPE_HALF:].reshape(Q_LORA, -1)], axis=1).astype(BF16)
    wuk = w_uk.reshape(KV_LORA, nh * NOPE_DIM).astype(BF16)
    rw = nh * ROPE_HALF
    grp = np.arange(rw) // ROPE_HALF
    ones_bd = jnp.asarray(grp[:, None] == grp[None, :], BF16)
    perm = np.zeros((2 * rw, nh * LANES), np.float32)
    for h in range(nh):
        for j in range(ROPE_HALF):
            perm[h * ROPE_HALF + j, h * LANES + j] = 1.0
            perm[rw + h * ROPE_HALF + j, h * LANES + ROPE_HALF + j] = 1.0
    gkr_pad = jnp.concatenate([g_kr, jnp.zeros((LANES - ROPE_DIM,), g_kr.dtype)]).reshape(1, LANES)
    consts = (gmix.reshape(1, d), wa_pad, g_cq.reshape(1, -1), g_ckv.reshape(1, -1), gkr_pad,
              wuq, g_qn.reshape(1, -1), jnp.tile(g_qr[:ROPE_HALF], nh).reshape(1, rw),
              jnp.tile(g_qr[ROPE_HALF:], nh).reshape(1, rw), wuk, g_kn.reshape(1, -1),
              ones_bd, jnp.asarray(perm, BF16))
    ckv, kr, cbf, qcat, kcat = _mla_proj(x, cos, sin, consts)
    tp = b * l
    hw = nh * HEAD_PAD
    o_p = _attn_prompt(qcat, kcat, cbf, b, l)
    c_s = ckv[tp:].reshape(nb, ds, KV_LORA)
    kr_s = kr[tp:].reshape(nb, ds, ROPE_DIM)
    wukt = w_uk.reshape(KV_LORA, nh * NOPE_DIM).T.astype(BF16)
    o_s = _attn_sample(page_table, qcat[tp:].astype(F32).reshape(nb, ds, hw), c_s, kr_s.transpose(0, 2, 1),
                       wukt, g_kn.reshape(1, -1), cache_ckv, cache_krope_t, layer=layer)
    attn = (o_p, o_s, w_uv.transpose(1, 0, 2).astype(BF16), w_o.astype(BF16))
    return (attn, ckv[:tp].reshape(b, l, KV_LORA), kr[:tp].reshape(b, l, ROPE_DIM), c_s, kr_s)


def kernel(x_prompt, x_sample, cache_ckv, cache_krope, state_s5_re, state_s5_im, page_table, norm_ffn1, ffn1_w_gate, ffn1_w_up, ffn1_w_down, norm_mix, norm_ffn2, ffn2_w_gate, ffn2_w_up, ffn2_w_down, ab_w_in, gm_norm_g, gm_norm_b, gm_w_s, gm_b_s, s5_lambda_re, s5_lambda_im, s5_log_dt, s5_b_re, s5_b_im, s5_c_re, s5_c_im, s5_d, s5_w_glu, s5_b_glu, ab_w_out, mla_w_a, mla_g_cq, mla_g_ckv, mla_w_uq, mla_g_qn, mla_g_qr, mla_g_kr, mla_w_uk, mla_g_kn, mla_w_uv, mla_w_o):
    b, l, d = x_prompt.shape
    nb, ds, _ = x_sample.shape
    depth = norm_ffn1.shape[0]
    past = page_table.shape[1] * cache_ckv.shape[2]
    x = (x_prompt.reshape(b * l, d), x_sample.reshape(nb * ds, d))
    cos, sin = _rope_tables(b, l, nb, ds, past)
    cache_krope_t = jnp.swapaxes(cache_krope, 2, 3)
    ckv_p, kr_p, ckv_s, kr_s = [], [], [], []
    s5r_p, s5i_p, s5r_s, s5i_s, gmv_s = [], [], [], [], []
    def ffn1(layer):
        return (norm_ffn1[layer], ffn1_w_gate[layer].astype(BF16), ffn1_w_up[layer].astype(BF16),
                ffn1_w_down[layer].astype(BF16))

    def ffn2(layer):
        return (norm_ffn2[layer], ffn2_w_gate[layer].astype(BF16), ffn2_w_up[layer].astype(BF16),
                ffn2_w_down[layer].astype(BF16))

    pending = []
    for layer in range(depth):
        x = _ffn(x, pending + [ffn1(layer)])
        pending = []
        if layer % 2 == 0:
            i = layer // 2
            x, v_s, hr_p, hi_p, hr_s, hi_s = _even_layer(
                x, b, l, nb, ds, state_s5_re[i], state_s5_im[i], norm_mix[layer], ab_w_in[i],
                gm_norm_g[i], gm_norm_b[i], gm_w_s[i], gm_b_s[i], s5_lambda_re[i], s5_lambda_im[i],
                s5_log_dt[i], s5_b_re[i], s5_b_im[i], s5_c_re[i], s5_c_im[i], s5_d[i], s5_w_glu[i],
                s5_b_glu[i], ab_w_out[i])
            s5r_p.append(hr_p)
            s5i_p.append(hi_p)
            s5r_s.append(hr_s)
            s5i_s.append(hi_s)
            gmv_s.append(v_s)
            pending = [ffn2(layer)]
        else:
            j = layer // 2
            attn, c_p, k_p, c_s, k_s = _mla_layer(
                x, b, l, nb, ds, j, cos, sin, norm_mix[layer], cache_ckv, cache_krope_t, page_table,
                mla_w_a[j], mla_g_cq[j], mla_g_ckv[j], mla_w_uq[j], mla_g_qn[j], mla_g_qr[j],
                mla_g_kr[j], mla_w_uk[j], mla_g_kn[j], mla_w_uv[j], mla_w_o[j])
            ckv_p.append(c_p)
            kr_p.append(k_p)
            ckv_s.append(c_s)
            kr_s.append(k_s)
            x = _ffn(x, [ffn2(layer)], attn=attn, split_rows=b * l if layer == depth - 1 else None)
    if pending:
        x = _ffn(x, pending, split_rows=b * l)
    xp = x[0].reshape(b, l, d)
    xs = x[1].reshape(nb, ds, d)
    return (xp, xs,
            jnp.stack(ckv_p), jnp.stack(kr_p), jnp.stack(ckv_s), jnp.stack(kr_s),
            jnp.stack(s5r_p), jnp.stack(s5i_p), jnp.stack(s5r_s), jnp.stack(s5i_s),
            jnp.stack(gmv_s))
```

```python
import functools
import math

import jax
import jax.numpy as jnp
import numpy as np
from jax import lax
from jax.experimental import pallas as pl
from jax.experimental.pallas import tpu as pltpu

F32 = jnp.float32
BF16 = jnp.bfloat16

EPS = 1e-6
GM_HEADS = 4
CHUNK = 128
S5_CH = 16
S5_STATE = 64
S5_QCHUNKS = 4
MLA_HEADS = 8
Q_LORA = 384
KV_LORA = 256
NOPE_DIM = 128
ROPE_DIM = 64
ROPE_HALF = ROPE_DIM // 2
V_DIM = 128
ROPE_BASE = 10000.0
ATTN_SCALE = (NOPE_DIM + ROPE_DIM) ** -0.5
Q_SCALE = ATTN_SCALE * math.log2(math.e)
HEAD_PAD = 256
LANES = 128
SUBLANES = 8
VMEM_LIMIT = 56 * 1024 * 1024


def _pick(n, prefs):
    for p in prefs:
        if n % p == 0:
            return p
    raise ValueError(f"no tile in {prefs} divides {n}")


def _params(*sem):
    return pltpu.CompilerParams(dimension_semantics=sem, vmem_limit_bytes=VMEM_LIMIT)


def _const_spec(shape):
    n = len(shape)
    return pl.BlockSpec(shape, lambda *_: (0,) * n, pipeline_mode=pl.Buffered(1))


def _rms(x, g):
    return x * lax.rsqrt(jnp.mean(x * x, axis=-1, keepdims=True) + EPS) * g


def _sigmoid(x):
    return 1.0 / (1.0 + jnp.exp(-x))


def _gelu_tanh(x):
    return 0.5 * x * (1.0 + jnp.tanh(math.sqrt(2.0 / math.pi) * (x + 0.044715 * (x * x * x))))


def _dot(a, b):
    return jnp.dot(a, b, preferred_element_type=F32)


def _lane_tile(x, n):
    return jnp.concatenate([x] * n, axis=1)


def _dot_nt(a, b):
    return lax.dot_general(a, b, (((1,), (1,)), ((), ())), preferred_element_type=F32)


def _ffn_body(*refs, fchunk, n_in, n_stage, has_attn, n_out, n_head_tiles):
    x_refs = refs[:n_in]
    k = n_in
    if has_attn:
        op_ref, os_ref, wuv_ref, wo_ref = refs[k:k + 4]
        k += 4
    stages = [refs[k + 4 * j:k + 4 * j + 4] for j in range(n_stage)]
    k += 4 * n_stage
    o_refs = refs[k:k + n_out]
    a_scr = refs[k + n_out]
    i = pl.program_id(0)
    if n_in == 1:
        x = x_refs[0][...]
    else:
        x = jnp.where(i < n_head_tiles, x_refs[0][...], x_refs[1][...])

    if has_attn:
        v_scr = refs[k + n_out + 1]

        def up_project(o_ref):
            for h in range(MLA_HEADS):
                v_scr[:, h * V_DIM:(h + 1) * V_DIM] = _dot(
                    o_ref[:, h * KV_LORA:(h + 1) * KV_LORA].astype(BF16), wuv_ref[h]).astype(BF16)

        @pl.when(i < n_head_tiles)
        def _():
            up_project(op_ref)

        @pl.when(i >= n_head_tiles)
        def _():
            up_project(os_ref)

        x = x + _dot(v_scr[...], wo_ref[...])

    for g_ref, wg_ref, wu_ref, wd_ref in stages:
        h = _rms(x, g_ref[...]).astype(BF16)
        d_ff = wg_ref.shape[1]
        for c in range(0, d_ff, fchunk):
            g = _dot(h, wg_ref[:, c:c + fchunk])
            u = _dot(h, wu_ref[:, c:c + fchunk])
            a_scr[:, c:c + fchunk] = (g * _sigmoid(g) * u).astype(BF16)
        x = x + 0.5 * _dot(a_scr[...], wd_ref[...])

    if n_out == 1:
        o_refs[0][...] = x
    else:
        @pl.when(i < n_head_tiles)
        def _():
            o_refs[0][...] = x

        @pl.when(i >= n_head_tiles)
        def _():
            o_refs[1][...] = x


def _ffn(xs, stages, attn=None, split_rows=None):
    xs = xs if isinstance(xs, tuple) else (xs,)
    d = xs[0].shape[1]
    t = sum(x.shape[0] for x in xs)
    d_ff = stages[0][1].shape[1]
    if len(xs) == 2:
        head = xs[0].shape[0]
    elif attn is not None:
        head = attn[0].shape[0]
    else:
        head = split_rows or t
    tm = _pick(math.gcd(head, t - head) if head < t else t, (512, 256, 128, 64, 32, 16, 8))
    n_head = head // tm
    head_map = lambda i: (jnp.minimum(i, n_head - 1), 0)
    tail_map = lambda i: (jnp.maximum(i - n_head, 0), 0)
    row_map = lambda i: (i, 0)
    args = list(xs)
    in_specs = ([pl.BlockSpec((tm, d), row_map)] if len(xs) == 1 else
                [pl.BlockSpec((tm, d), head_map), pl.BlockSpec((tm, d), tail_map)])
    scratch = [pltpu.VMEM((tm, d_ff), BF16)]
    if attn is not None:
        o_head, o_tail, wuv, wo = attn
        args += [o_head, o_tail, wuv, wo]
        in_specs += [pl.BlockSpec((tm, o_head.shape[1]), head_map),
                     pl.BlockSpec((tm, o_tail.shape[1]), tail_map),
                     _const_spec(wuv.shape), _const_spec(wo.shape)]
        scratch.append(pltpu.VMEM((tm, MLA_HEADS * V_DIM), BF16))
    for g, wg, wu, wd in stages:
        args += [g.reshape(1, d), wg, wu, wd]
        in_specs += [_const_spec((1, d)), _const_spec(wg.shape), _const_spec(wu.shape), _const_spec(wd.shape)]
    if split_rows is None:
        out_shape = jax.ShapeDtypeStruct((t, d), F32)
        out_specs = pl.BlockSpec((tm, d), row_map)
    else:
        out_shape = (jax.ShapeDtypeStruct((head, d), F32), jax.ShapeDtypeStruct((t - head, d), F32))
        out_specs = (pl.BlockSpec((tm, d), head_map), pl.BlockSpec((tm, d), tail_map))
    return pl.pallas_call(
        functools.partial(_ffn_body, fchunk=256, n_in=len(xs), n_stage=len(stages), has_attn=attn is not None,
                          n_out=1 if split_rows is None else 2, n_head_tiles=n_head),
        out_shape=out_shape,
        grid=(t // tm,),
        in_specs=in_specs,
        out_specs=out_specs,
        scratch_shapes=scratch,
        compiler_params=_params("arbitrary"),
        name="ffn",
    )(*args)


def _s5_tables(lam_re, lam_im, log_dt, b_re, b_im, c_re, c_im):
    lr, li = lam_re.astype(F32), lam_im.astype(F32)
    dt = jnp.exp(log_dt.astype(F32))[:, None]
    mag = jnp.exp(lr * dt)
    ar, ai = mag * jnp.cos(li * dt), mag * jnp.sin(li * dt)
    er, ei = ar - 1.0, ai
    den = lr * lr + li * li
    qr = (er * lr + ei * li) / den
    qi = (ei * lr - er * li) / den
    br, bi = b_re.astype(F32), b_im.astype(F32)
    bbr = qr[..., None] * br - qi[..., None] * bi
    bbi = qr[..., None] * bi + qi[..., None] * br
    g = lr.shape[0]
    gq = g // S5_QCHUNKS
    eye = jnp.eye(gq, dtype=F32)

    def b_blk(bb):
        bq = bb.reshape(S5_QCHUNKS, gq, S5_STATE, S5_CH)
        return jnp.einsum('qgnc,gh->qhcgn', bq, eye).reshape(S5_QCHUNKS, gq * S5_CH, gq * S5_STATE)

    def c_blk(cc):
        cq = cc.astype(F32).reshape(S5_QCHUNKS, gq, S5_CH, S5_STATE)
        return jnp.einsum('qgcn,gh->qgnhc', cq, eye).reshape(S5_QCHUNKS, gq * S5_STATE, gq * S5_CH)

    bblk = jnp.concatenate([b_blk(bbr), b_blk(bbi)], axis=2).astype(BF16)
    cblk = jnp.concatenate([c_blk(c_re), -c_blk(c_im)], axis=1).astype(BF16)
    a_tab = jnp.stack([ar.reshape(S5_QCHUNKS, -1), ai.reshape(S5_QCHUNKS, -1)], axis=1)
    return a_tab, bblk, cblk


def _pack_state(hr, hi):
    n = hr.shape[0]
    r = hr.astype(F32).reshape(n, S5_QCHUNKS, 1, -1)
    i = hi.astype(F32).reshape(n, S5_QCHUNKS, 1, -1)
    return jnp.concatenate([r, i], axis=2).reshape(n, -1)


def _unpack_state(st, groups):
    n = st.shape[0]
    s4 = st.reshape(n, S5_QCHUNKS, 2, -1)
    return (s4[:, :, 0].reshape(n, groups, S5_STATE), s4[:, :, 1].reshape(n, groups, S5_STATE))


def _cmul(ar, ai, br, bi):
    return ar * br - ai * bi, ar * bi + ai * br


def _even_prompt_body(x_ref, gmix_ref, win_ref, lng_ref, lnb_ref, wtril_ref, sbias_ref,
                      atab_ref, bblk_ref, cblk_ref, d_ref, wglu_ref, bglu_ref, wout_ref,
                      o_ref, st_ref,
                      us_scr, usp_scr, xs_scr, yp_scr, y_scr, cat_scr, carry_scr, *, tt):
    m = tt // SUBLANES
    gw = lng_ref.shape[1]
    sw = d_ref.shape[1]
    hs = sw
    t_idx = pl.program_id(1)

    @pl.when(t_idx == 0)
    def _():
        carry_scr[...] = jnp.zeros_like(carry_scr)

    x = x_ref[...]
    hn = _rms(x, gmix_ref[...]).astype(BF16)
    z = _dot(hn, win_ref[...])
    u = _gelu_tanh(z[:, :gw])
    v = _gelu_tanh(z[:, gw:2 * gw])
    mu = jnp.mean(v, axis=-1, keepdims=True)
    vc = v - mu
    v = vc * lax.rsqrt(jnp.mean(vc * vc, axis=-1, keepdims=True) + EPS) * lng_ref[...] + lnb_ref[...]
    vb = v.astype(BF16)
    hd = gw // GM_HEADS
    for c in range(tt // CHUNK):
        rows = slice(c * CHUNK, (c + 1) * CHUNK)
        for h in range(GM_HEADS):
            cols = slice(h * hd, (h + 1) * hd)
            s = _dot(wtril_ref[h], vb[rows, cols]) + sbias_ref[:, cols]
            cat_scr[rows, cols] = (u[rows, cols] * s).astype(BF16)

    qw = sw // S5_QCHUNKS
    for q in range(S5_QCHUNKS):
        us_scr[q] = z[:, 2 * gw + q * qw:2 * gw + (q + 1) * qw]
    sub = lax.broadcasted_iota(jnp.int32, (SUBLANES, hs), 0)

    def shift_rows(val, k):
        return jnp.where(sub >= k, pltpu.roll(val, k, axis=0), 0.0)

    for q in range(S5_QCHUNKS):
        for i in range(m):
            usp_scr[i * SUBLANES:(i + 1) * SUBLANES, :] = us_scr.at[q][pl.ds(i, SUBLANES, stride=m), :]
        xs_scr[...] = _dot(usp_scr[...].astype(BF16), bblk_ref[q])
        ar = atab_ref[q, 0:1, :]
        ai = atab_ref[q, 1:2, :]
        arb = jnp.broadcast_to(ar, (SUBLANES, hs))
        aib = jnp.broadcast_to(ai, (SUBLANES, hs))

        def step(i, carry, store):
            hr, hi = carry
            r0 = pl.multiple_of(i * SUBLANES, SUBLANES)
            xr = xs_scr[pl.ds(r0, SUBLANES), 0:hs]
            xi = xs_scr[pl.ds(r0, SUBLANES), hs:2 * hs]
            nr = arb * hr - aib * hi + xr
            ni = arb * hi + aib * hr + xi
            if store:
                xs_scr[pl.ds(r0, SUBLANES), 0:hs] = nr
                xs_scr[pl.ds(r0, SUBLANES), hs:2 * hs] = ni
            return nr, ni

        zero = jnp.zeros((SUBLANES, hs), F32)
        er, ei = lax.fori_loop(0, m, functools.partial(step, store=False), (zero, zero), unroll=True)

        pr, pi = ar, ai
        for _ in range(int(math.log2(m))):
            pr, pi = _cmul(pr, pi, pr, pi)
        a1 = (pr, pi)
        a2 = _cmul(*a1, *a1)
        a4 = _cmul(*a2, *a2)
        fr, fi = er, ei
        for k, (kr, ki) in ((1, a1), (2, a2), (4, a4)):
            sr, si = shift_rows(fr, k), shift_rows(fi, k)
            dr, di = _cmul(kr, ki, sr, si)
            fr, fi = fr + dr, fi + di
        wr, wi = jnp.ones((SUBLANES, hs), F32), jnp.zeros((SUBLANES, hs), F32)
        for bit, (kr, ki) in ((1, a1), (2, a2), (4, a4)):
            sel = (sub & bit) != 0
            wr, wi = _cmul(wr, wi, jnp.where(sel, kr, 1.0), jnp.where(sel, ki, 0.0))
        cr = carry_scr[q, SUBLANES - 1:SUBLANES, 0:hs]
        ci = carry_scr[q, SUBLANES - 1:SUBLANES, hs:2 * hs]
        tr, ti = _cmul(wr, wi, cr, ci)
        h0r = tr + shift_rows(fr, 1)
        h0i = ti + shift_rows(fi, 1)
        nr, ni = _cmul(a1[0], a1[1], h0r, h0i)
        carry_scr[q, :, 0:hs] = nr + er
        carry_scr[q, :, hs:2 * hs] = ni + ei
        st_ref[0, :, q * 2 * hs:(q + 1) * 2 * hs] = carry_scr[q, SUBLANES - 1:SUBLANES, :]

        lax.fori_loop(0, m, functools.partial(step, store=True), (h0r, h0i), unroll=True)
        yp_scr[...] = _dot(xs_scr[...].astype(BF16), cblk_ref[q])
        for s in range(SUBLANES):
            y_scr[s * m:(s + 1) * m, q * qw:(q + 1) * qw] = (
                yp_scr[pl.ds(s, m, stride=SUBLANES), :]
                + d_ref[:, q * qw:(q + 1) * qw] * us_scr[q, s * m:(s + 1) * m, :])

    g = _gelu_tanh(y_scr[...])
    gate = _sigmoid(_dot(g.astype(BF16), wglu_ref[...]) + bglu_ref[...])
    cat_scr[:, gw:gw + sw] = (g * gate).astype(BF16)
    o_ref[...] = x + _dot(cat_scr[...], wout_ref[...])


def _even_prompt(x, b, l, gmix, win, lng, lnb, wtril, sbias, atab, bblk, cblk, dvec, wglu, bglu, wout):
    d = x.shape[1]
    gw, sw = lng.shape[1], dvec.shape[1]
    tt = _pick(l, (256, 128))
    nt = l // tt
    hs2 = 2 * sw
    args = (x, gmix, win, lng, lnb, wtril, sbias, atab, bblk, cblk, dvec, wglu, bglu, wout)
    in_specs = [pl.BlockSpec((tt, d), lambda bi, ti: (bi * nt + ti, 0))]
    in_specs += [_const_spec(a.shape) for a in args[1:]]
    return pl.pallas_call(
        functools.partial(_even_prompt_body, tt=tt),
        out_shape=(jax.ShapeDtypeStruct((b * l, d), F32),
                   jax.ShapeDtypeStruct((b, 1, S5_QCHUNKS * hs2), F32)),
        grid=(b, nt),
        in_specs=in_specs,
        out_specs=(pl.BlockSpec((tt, d), lambda bi, ti: (bi * nt + ti, 0)),
                   pl.BlockSpec((1, 1, S5_QCHUNKS * hs2), lambda bi, ti: (bi, 0, 0))),
        scratch_shapes=[
            pltpu.VMEM((S5_QCHUNKS, tt, sw // S5_QCHUNKS), F32),
            pltpu.VMEM((tt, sw // S5_QCHUNKS), F32),
            pltpu.VMEM((tt, hs2), F32),
            pltpu.VMEM((tt, sw // S5_QCHUNKS), F32),
            pltpu.VMEM((tt, sw), F32),
            pltpu.VMEM((tt, gw + sw), BF16),
            pltpu.VMEM((S5_QCHUNKS, SUBLANES, hs2), F32),
        ],
        compiler_params=_params("arbitrary", "arbitrary"),
        name="even_prompt",
    )(*args)


def _even_sample_body(x_ref, h0_ref, gmix_ref, win_ref, lng_ref, lnb_ref, scoef_ref, sbias_ref,
                      atab_ref, bblk_ref, cblk_ref, d_ref, wglu_ref, bglu_ref, wout_ref,
                      o_ref, v_ref, st_ref,
                      xs_scr, y_scr, cat_scr, *, ds, nb):
    gw = lng_ref.shape[1]
    sw = d_ref.shape[1]
    hs = sw
    x = x_ref[...]
    hn = _rms(x, gmix_ref[...]).astype(BF16)
    z = _dot(hn, win_ref[...])
    u = _gelu_tanh(z[:, :gw])
    v = _gelu_tanh(z[:, gw:2 * gw])
    mu = jnp.mean(v, axis=-1, keepdims=True)
    vc = v - mu
    v = vc * lax.rsqrt(jnp.mean(vc * vc, axis=-1, keepdims=True) + EPS) * lng_ref[...] + lnb_ref[...]
    v_ref[...] = v
    for t in range(ds):
        s = jnp.broadcast_to(sbias_ref[t:t + 1, :], (nb, gw))
        for sp in range(t + 1):
            s = s + scoef_ref[t * ds + sp:t * ds + sp + 1, :] * v[sp * nb:(sp + 1) * nb, :]
        cat_scr[t * nb:(t + 1) * nb, 0:gw] = (u[t * nb:(t + 1) * nb, :] * s).astype(BF16)

    us = z[:, 2 * gw:]
    usb = us.astype(BF16)
    qw = sw // S5_QCHUNKS
    for q in range(S5_QCHUNKS):
        xs_scr[...] = _dot(usb[:, q * qw:(q + 1) * qw], bblk_ref[q])
        ar = atab_ref[q, 0:1, :]
        ai = atab_ref[q, 1:2, :]
        hr = h0_ref[:, q * 2 * hs:q * 2 * hs + hs]
        hi = h0_ref[:, q * 2 * hs + hs:(q + 1) * 2 * hs]
        for t in range(ds):
            rows = slice(t * nb, (t + 1) * nb)
            nr = ar * hr - ai * hi + xs_scr[rows, 0:hs]
            ni = ar * hi + ai * hr + xs_scr[rows, hs:2 * hs]
            xs_scr[rows, 0:hs] = nr
            xs_scr[rows, hs:2 * hs] = ni
            hr, hi = nr, ni
        st_ref[:, q * 2 * hs:q * 2 * hs + hs] = hr
        st_ref[:, q * 2 * hs + hs:(q + 1) * 2 * hs] = hi
        y_scr[:, q * qw:(q + 1) * qw] = _dot(xs_scr[...].astype(BF16), cblk_ref[q])
    g = _gelu_tanh(y_scr[...] + d_ref[...] * us)
    gate = _sigmoid(_dot(g.astype(BF16), wglu_ref[...]) + bglu_ref[...])
    cat_scr[:, gw:gw + sw] = (g * gate).astype(BF16)
    o_ref[...] = x + _dot(cat_scr[...], wout_ref[...])


def _even_sample(xs_tm, h0, gmix, win, lng, lnb, scoef, sbias8, atab, bblk, cblk, dvec, wglu, bglu, wout, *, ds):
    t, d = xs_tm.shape
    nb = t // ds
    gw, sw = lng.shape[1], dvec.shape[1]
    args = (xs_tm, h0, gmix, win, lng, lnb, scoef, sbias8, atab, bblk, cblk, dvec, wglu, bglu, wout)
    return pl.pallas_call(
        functools.partial(_even_sample_body, ds=ds, nb=nb),
        out_shape=(jax.ShapeDtypeStruct((t, d), F32),
                   jax.ShapeDtypeStruct((t, gw), F32),
                   jax.ShapeDtypeStruct(h0.shape, F32)),
        grid=(1,),
        in_specs=[_const_spec(a.shape) for a in args],
        out_specs=tuple(pl.BlockSpec(s, lambda *_: (0, 0)) for s in ((t, d), (t, gw), h0.shape)),
        scratch_shapes=[
            pltpu.VMEM((t, 2 * sw), F32),
            pltpu.VMEM((t, sw), F32),
            pltpu.VMEM((t, gw + sw), BF16),
        ],
        compiler_params=_params("arbitrary"),
        name="even_sample",
    )(*args)


def _mla_proj_body(x_ref, cos_ref, sin_ref, gmix_ref, wa_ref, gcq_ref, gckv_ref, gkr_ref,
                   wuq_ref, gqn_ref, gq1_ref, gq2_ref, wuk_ref, gkn_ref, ones_ref, perm_ref,
                   ckv_ref, kr_ref, cbf_ref, qcat_ref, kcat_ref):
    tm = x_ref.shape[0]
    x = x_ref[...]
    hn = _rms(x, gmix_ref[...]).astype(BF16)
    a = _dot(hn, wa_ref[...])
    cq = _rms(a[:, :Q_LORA], gcq_ref[...])
    ckv = _rms(a[:, Q_LORA:Q_LORA + KV_LORA], gckv_ref[...])
    ckv_ref[...] = ckv
    ckv_b = ckv.astype(BF16)
    cbf_ref[...] = ckv_b

    cos = cos_ref[...]
    sin = sin_ref[...]
    slab = a[:, Q_LORA + KV_LORA:]
    ms = jnp.sum(slab * slab, axis=-1, keepdims=True) * (1.0 / ROPE_DIM)
    kn = slab * lax.rsqrt(ms + EPS) * gkr_ref[...]
    lane = lax.broadcasted_iota(jnp.int32, (tm, LANES), 1)
    rot = jnp.where(lane < ROPE_HALF, -pltpu.roll(kn, LANES - ROPE_HALF, axis=1),
                    pltpu.roll(kn, ROPE_HALF, axis=1))
    kr = kn * cos + rot * sin
    kr_ref[...] = kr[:, :ROPE_DIM]
    kr_pad = jnp.where(lane < ROPE_DIM, kr, 0.0).astype(BF16)

    q = _dot(cq.astype(BF16), wuq_ref[...])
    k = _dot(ckv_b, wuk_ref[...])
    nn = MLA_HEADS * NOPE_DIM
    rw = MLA_HEADS * ROPE_HALF
    x1 = q[:, nn:nn + rw]
    x2 = q[:, nn + rw:nn + 2 * rw]
    ssq = x1 * x1 + x2 * x2
    ssq_hi = ssq.astype(BF16)
    ssq_lo = (ssq - ssq_hi.astype(F32)).astype(BF16)
    gsum = _dot(ssq_hi, ones_ref[...]) + _dot(ssq_lo, ones_ref[...])
    r = lax.rsqrt(gsum * (1.0 / ROPE_DIM) + EPS)
    x1n = x1 * r * gq1_ref[...]
    x2n = x2 * r * gq2_ref[...]
    reps = rw // LANES
    cos2 = jnp.concatenate([cos] * reps, axis=1)
    sin2 = jnp.concatenate([sin] * reps, axis=1)
    o1 = (x1n * cos2 - x2n * sin2) * Q_SCALE
    o2 = (x2n * cos2 + x1n * sin2) * Q_SCALE
    qr_cat = _dot(jnp.concatenate([o1, o2], axis=1).astype(BF16), perm_ref[...]).astype(BF16)
    for h in range(MLA_HEADS):
        nsl = slice(h * NOPE_DIM, (h + 1) * NOPE_DIM)
        qcat_ref[:, h * HEAD_PAD:h * HEAD_PAD + NOPE_DIM] = (_rms(q[:, nsl], gqn_ref[...]) * Q_SCALE).astype(BF16)
        qcat_ref[:, h * HEAD_PAD + NOPE_DIM:(h + 1) * HEAD_PAD] = qr_cat[:, h * LANES:(h + 1) * LANES]
        kcat_ref[:, h * HEAD_PAD:h * HEAD_PAD + NOPE_DIM] = _rms(k[:, nsl], gkn_ref[...]).astype(BF16)
        kcat_ref[:, h * HEAD_PAD + NOPE_DIM:(h + 1) * HEAD_PAD] = kr_pad


def _mla_proj(x, cos, sin, consts):
    t, d = x.shape
    tm = _pick(t, (256, 128, 64, 8))
    hw = MLA_HEADS * HEAD_PAD
    row = lambda w: pl.BlockSpec((tm, w), lambda i: (i, 0))
    return pl.pallas_call(
        _mla_proj_body,
        out_shape=(jax.ShapeDtypeStruct((t, KV_LORA), F32),
                   jax.ShapeDtypeStruct((t, ROPE_DIM), F32),
                   jax.ShapeDtypeStruct((t, KV_LORA), BF16),
                   jax.ShapeDtypeStruct((t, hw), BF16),
                   jax.ShapeDtypeStruct((t, hw), BF16)),
        grid=(t // tm,),
        in_specs=[row(d), row(LANES), row(LANES)] + [_const_spec(c.shape) for c in consts],
        out_specs=(row(KV_LORA), row(ROPE_DIM), row(KV_LORA), row(hw), row(hw)),
        compiler_params=_params("arbitrary"),
        name="mla_proj",
    )(x, cos, sin, *consts)


def _attn_prompt_body(qi_ref, kj_ref, q_ref, k_ref, c_ref, o_ref, m_scr, l_scr, acc_scr, *, tq):
    p = pl.program_id(1)
    qi = qi_ref[p]
    kj = kj_ref[p]

    @pl.when(kj == 0)
    def _():
        m_scr[...] = jnp.full_like(m_scr, -jnp.inf)
        l_scr[...] = jnp.zeros_like(l_scr)
        acc_scr[...] = jnp.zeros_like(acc_scr)

    def tile(sub, masked):
        ksl = slice(sub * tq, (sub + 1) * tq)
        c = c_ref[ksl, :]
        nrep = tq // LANES
        vrep = KV_LORA // LANES
        if masked:
            row = lax.broadcasted_iota(jnp.int32, (tq, tq), 0)
            col = lax.broadcasted_iota(jnp.int32, (tq, tq), 1)
            keep = row >= col
        for h in range(MLA_HEADS):
            hsl = slice(h * HEAD_PAD, (h + 1) * HEAD_PAD)
            s = _dot_nt(q_ref[:, hsl], k_ref[ksl, hsl])
            if masked:
                s = jnp.where(keep, s, -jnp.inf)
            m_prev = m_scr[h]
            m_new = jnp.maximum(m_prev, jnp.max(s, axis=-1, keepdims=True))
            alpha = jnp.exp2(m_prev - m_new)
            pexp = jnp.exp2(s - _lane_tile(m_new, nrep))
            psum = pexp[:, 0:LANES]
            for j in range(1, nrep):
                psum = psum + pexp[:, j * LANES:(j + 1) * LANES]
            l_scr[h] = alpha * l_scr[h] + psum
            acc_scr[h] = _lane_tile(alpha, vrep) * acc_scr[h] + _dot(pexp.astype(BF16), c)
            m_scr[h] = m_new

    def finalize():
        for h in range(MLA_HEADS):
            o_ref[:, h * KV_LORA:(h + 1) * KV_LORA] = (
                acc_scr[h] / jnp.sum(l_scr[h], axis=-1, keepdims=True)).astype(BF16)

    @pl.when(2 * kj + 1 < qi)
    def _():
        tile(0, False)
        tile(1, False)

    @pl.when(2 * kj + 1 == qi)
    def _():
        tile(0, False)
        tile(1, True)
        finalize()

    @pl.when(2 * kj == qi)
    def _():
        tile(0, True)
        finalize()


def _attn_prompt(qcat, kcat, cbf, b, l):
    hw = qcat.shape[1]
    tq = _pick(l, (512, 256, 128))
    nq = l // tq
    nkb = nq // 2
    pairs = [(i, j) for i in range(nq) for j in range(i // 2 + 1)]
    qi_tab = jnp.asarray([i for i, _ in pairs], jnp.int32)
    ki_tab = jnp.asarray([j for _, j in pairs], jnp.int32)
    grid_spec = pltpu.PrefetchScalarGridSpec(
        num_scalar_prefetch=2,
        grid=(b, len(pairs)),
        in_specs=[
            pl.BlockSpec((tq, hw), lambda bi, p, qi, kj: (bi * nq + qi[p], 0)),
            pl.BlockSpec((2 * tq, hw), lambda bi, p, qi, kj: (bi * nkb + kj[p], 0)),
            pl.BlockSpec((2 * tq, KV_LORA), lambda bi, p, qi, kj: (bi * nkb + kj[p], 0)),
        ],
        out_specs=pl.BlockSpec((tq, MLA_HEADS * KV_LORA), lambda bi, p, qi, kj: (bi * nq + qi[p], 0)),
        scratch_shapes=[
            pltpu.VMEM((MLA_HEADS, tq, LANES), F32),
            pltpu.VMEM((MLA_HEADS, tq, LANES), F32),
            pltpu.VMEM((MLA_HEADS, tq, KV_LORA), F32),
        ],
    )
    return pl.pallas_call(
        functools.partial(_attn_prompt_body, tq=tq),
        out_shape=jax.ShapeDtypeStruct((b * l, MLA_HEADS * KV_LORA), BF16),
        grid_spec=grid_spec,
        compiler_params=_params("arbitrary", "arbitrary"),
        name="attn_prompt",
    )(qi_tab, ki_tab, qcat, kcat, cbf)


def _attn_sample_body(pt_ref, q_ref, cn_ref, krn_ref, wukt_ref, gkn_ref, ckv_hbm, krc_hbm,
                      o_ref, cbuf, krbuf, sems, lhs_scr, big_scr, tail_c, tail_kr,
                      *, layer, n_pages, page, tk, ds):
    b = pl.program_id(0)
    nb = pl.num_programs(0)
    slot = lax.rem(b, 2)
    nh = MLA_HEADS
    nrow = nh * ds
    kw = nh * NOPE_DIM
    ppt = tk // page
    nt = n_pages // ppt

    def page_copies(seq, sl, j):
        pg = pt_ref[seq, j]
        r0 = pl.multiple_of(j * page, page)
        return (pltpu.make_async_copy(ckv_hbm.at[layer, pg], cbuf.at[sl, pl.ds(r0, page), :], sems.at[0, sl]),
                pltpu.make_async_copy(krc_hbm.at[layer, pg], krbuf.at[sl, j], sems.at[1, sl]))

    def start_fetch(seq, sl):
        def body(j, carry):
            for cp in page_copies(seq, sl, j):
                cp.start()
            return carry
        lax.fori_loop(0, n_pages, body, 0, unroll=8)

    @pl.when(b == 0)
    def _():
        start_fetch(0, 0)
        lhs_scr[0:kw, :] = wukt_ref[...]
        tail_c[...] = jnp.zeros_like(tail_c)
        tail_kr[...] = jnp.zeros_like(tail_kr)

    for j in range(n_pages):
        for cp in page_copies(b, slot, j):
            cp.wait()

    @pl.when(b + 1 < nb)
    def _():
        start_fetch(b + 1, 1 - slot)

    qf = q_ref[0]
    qabs, qrope = [], []
    for h in range(nh):
        qn = (qf[:, h * HEAD_PAD:h * HEAD_PAD + NOPE_DIM] * gkn_ref[...]).astype(BF16)
        qabs.append(_dot(qn, wukt_ref[h * NOPE_DIM:(h + 1) * NOPE_DIM, :]))
        qrope.append(qf[:, h * HEAD_PAD + NOPE_DIM:h * HEAD_PAD + NOPE_DIM + ROPE_DIM])
    lhs_scr[kw:kw + nrow, :] = jnp.concatenate(qabs, axis=0).astype(BF16)
    qr_b = jnp.concatenate(qrope, axis=0).astype(BF16)
    tail_c[0:ds, :] = cn_ref[0]
    tail_kr[:, 0:ds] = krn_ref[0]

    def softmax_update(k_rows, sraw, cb, krt, carry, width, masked):
        m_prev, l_prev, acc = carry
        rinv = []
        for h in range(nh):
            kt = k_rows(h)
            ss = jnp.sum(kt * kt, axis=0, keepdims=True) * (1.0 / NOPE_DIM)
            rinv.append(jnp.broadcast_to(lax.rsqrt(ss + EPS), (ds, width)))
        rinv = jnp.concatenate(rinv, axis=0)
        s = sraw * rinv + _dot(qr_b, krt.astype(BF16))
        if masked:
            qpos = lax.rem(lax.broadcasted_iota(jnp.int32, (nrow, width), 0), ds)
            kpos = lax.broadcasted_iota(jnp.int32, (nrow, width), 1)
            s = jnp.where(kpos <= qpos, s, -jnp.inf)
        m_new = jnp.maximum(m_prev, jnp.max(s, axis=-1, keepdims=True))
        alpha = jnp.exp2(m_prev - m_new)
        pexp = jnp.exp2(s - m_new)
        l_new = alpha * l_prev + jnp.sum(pexp, axis=-1, keepdims=True)
        acc = alpha * acc + _dot(pexp.astype(BF16), cb)
        return m_new, l_new, acc

    def c_tile(t):
        return cbuf[slot, t * tk:(t + 1) * tk, :].astype(BF16)

    def scores_into(t, sb):
        big_scr[sb] = _dot_nt(lhs_scr[...], c_tile(t))

    def consume(t, sb, carry):
        krt = jnp.concatenate([krbuf[slot, t * ppt + d] for d in range(ppt)], axis=1)
        return softmax_update(lambda h: big_scr[sb, h * NOPE_DIM:(h + 1) * NOPE_DIM, :],
                              big_scr[sb, kw:kw + nrow, :], c_tile(t), krt, carry, tk, False)

    carry = (jnp.full((nrow, 1), -jnp.inf, F32), jnp.zeros((nrow, 1), F32),
             jnp.zeros((nrow, KV_LORA), F32))
    scores_into(0, 0)
    for t in range(nt):
        if t + 1 < nt:
            scores_into(t + 1, (t + 1) % 2)
        carry = consume(t, t % 2, carry)
    tcb = tail_c[...].astype(BF16)
    tbig = _dot_nt(lhs_scr[...], tcb)
    _, l_fin, acc = softmax_update(lambda h: tbig[h * NOPE_DIM:(h + 1) * NOPE_DIM, :],
                                   tbig[kw:kw + nrow, :], tcb, tail_kr[...], carry,
                                   tail_c.shape[0], True)
    out = acc / l_fin
    for h in range(nh):
        o_ref[:, h * KV_LORA:(h + 1) * KV_LORA] = out[h * ds:(h + 1) * ds, :]


def _attn_sample(page_table, q_s, c_new, krt_new, wukt, gkn, cache_ckv, cache_krope_t, *, layer):
    nb, ds, hw = q_s.shape
    n_pages = page_table.shape[1]
    page = cache_ckv.shape[2]
    tk = _pick(n_pages * page, (2048, 1024, 512, 256, 128))
    nrow = MLA_HEADS * ds
    kw = MLA_HEADS * NOPE_DIM
    grid_spec = pltpu.PrefetchScalarGridSpec(
        num_scalar_prefetch=1,
        grid=(nb,),
        in_specs=[
            pl.BlockSpec((1, ds, hw), lambda i, pt: (i, 0, 0)),
            pl.BlockSpec((1, ds, KV_LORA), lambda i, pt: (i, 0, 0)),
            pl.BlockSpec((1, ROPE_DIM, ds), lambda i, pt: (i, 0, 0)),
            pl.BlockSpec((kw, KV_LORA), lambda i, pt: (0, 0)),
            pl.BlockSpec((1, NOPE_DIM), lambda i, pt: (0, 0)),
            pl.BlockSpec(memory_space=pl.ANY),
            pl.BlockSpec(memory_space=pl.ANY),
        ],
        out_specs=pl.BlockSpec((ds, MLA_HEADS * KV_LORA), lambda i, pt: (i, 0)),
        scratch_shapes=[
            pltpu.VMEM((2, n_pages * page, KV_LORA), F32),
            pltpu.VMEM((2, n_pages, ROPE_DIM, page), F32),
            pltpu.SemaphoreType.DMA((2, 2)),
            pltpu.VMEM((kw + nrow, KV_LORA), BF16),
            pltpu.VMEM((2, kw + nrow, tk), F32),
            pltpu.VMEM((LANES, KV_LORA), F32),
            pltpu.VMEM((ROPE_DIM, LANES), F32),
        ],
    )
    return pl.pallas_call(
        functools.partial(_attn_sample_body, layer=layer, n_pages=n_pages, page=page, tk=tk, ds=ds),
        out_shape=jax.ShapeDtypeStruct((nb * ds, MLA_HEADS * KV_LORA), F32),
        grid_spec=grid_spec,
        compiler_params=_params("arbitrary"),
        name="attn_sample",
    )(page_table, q_s, c_new, krt_new, wukt, gkn, cache_ckv, cache_krope_t)


def _even_layer(x, b, l, nb, ds, h0r, h0i, gmix, w_in, gm_g, gm_b, gm_w_s, gm_b_s,
                lam_re, lam_im, log_dt, b_re, b_im, c_re, c_im, dvec, w_glu, b_glu, w_out):
    d = x.shape[1]
    gw = gm_g.shape[0]
    sw = dvec.shape[0]
    groups = lam_re.shape[0]
    atab, bblk, cblk = _s5_tables(lam_re, lam_im, log_dt, b_re, b_im, c_re, c_im)
    causal = jnp.tril(jnp.ones((CHUNK, CHUNK), dtype=bool))
    w_tril = jnp.where(causal[None], gm_w_s, 0.0)
    sbias = jnp.repeat(gm_b_s.T, gw // GM_HEADS, axis=1)
    scoef = jnp.repeat(w_tril[:, :ds, :ds].transpose(1, 2, 0).reshape(ds * ds, GM_HEADS),
                       gw // GM_HEADS, axis=1)
    common = (gmix.reshape(1, d), w_in.astype(BF16), gm_g.reshape(1, gw), gm_b.reshape(1, gw))
    tail = (atab, bblk, cblk, dvec.reshape(1, sw), w_glu.astype(BF16), b_glu.reshape(1, sw),
            w_out.astype(BF16))
    yp, st_p = _even_prompt(x, b, l, *common, w_tril.astype(BF16), sbias, *tail)
    xs_tm = x[b * l:].reshape(nb, ds, d).transpose(1, 0, 2).reshape(ds * nb, d)
    ys_tm, v_tm, st_s = _even_sample(xs_tm, _pack_state(h0r, h0i), *common, scoef, sbias[:ds], *tail, ds=ds)
    ys = ys_tm.reshape(ds, nb, d).transpose(1, 0, 2).reshape(nb * ds, d)
    v_s = v_tm.reshape(ds, nb, gw).transpose(1, 0, 2)
    x_new = (yp, ys)
    hr_p, hi_p = _unpack_state(st_p.reshape(b, -1), groups)
    hr_s, hi_s = _unpack_state(st_s, groups)
    return x_new, v_s, hr_p, hi_p, hr_s, hi_s


def _rope_tables(b, l, nb, ds, past):
    inv = ROPE_BASE ** (-jnp.arange(ROPE_HALF, dtype=F32) * (2.0 / ROPE_DIM))
    pos_p = jnp.arange(l, dtype=jnp.int32)
    pos_s = past + jnp.arange(ds, dtype=jnp.int32)
    pos = jnp.concatenate([jnp.tile(pos_p, b), jnp.tile(pos_s, nb)])
    ang = pos.astype(F32)[:, None] * inv[None, :]
    reps = LANES // ROPE_HALF
    return jnp.tile(jnp.cos(ang), (1, reps)), jnp.tile(jnp.sin(ang), (1, reps))


def _mla_layer(x, b, l, nb, ds, layer, cos, sin, gmix, cache_ckv, cache_krope_t, page_table,
               w_a, g_cq, g_ckv, w_uq, g_qn, g_qr, g_kr, w_uk, g_kn, w_uv, w_o):
    d = x.shape[1]
    nh = MLA_HEADS
    wa_pad = jnp.concatenate([w_a, jnp.zeros((d, LANES - ROPE_DIM), w_a.dtype)], axis=1).astype(BF16)
    wq3 = w_uq.reshape(Q_LORA, nh, NOPE_DIM + ROPE_DIM)
    wuq = jnp.concatenate([wq3[:, :, :NOPE_DIM].reshape(Q_LORA, -1),
                           wq3[:, :, NOPE_DIM:NOPE_DIM + ROPE_HALF].reshape(Q_LORA, -1),
                           wq3[:, :, NOPE_DIM + ROPE_HALF:].reshape(Q_LORA, -1)], axis=1).astype(BF16)
    wuk = w_uk.reshape(KV_LORA, nh * NOPE_DIM).astype(BF16)
    rw = nh * ROPE_HALF
    grp = np.arange(rw) // ROPE_HALF
    ones_bd = jnp.asarray(grp[:, None] == grp[None, :], BF16)
    perm = np.zeros((2 * rw, nh * LANES), np.float32)
    for h in range(nh):
        for j in range(ROPE_HALF):
            perm[h * ROPE_HALF + j, h * LANES + j] = 1.0
            perm[rw + h * ROPE_HALF + j, h * LANES + ROPE_HALF + j] = 1.0
    gkr_pad = jnp.concatenate([g_kr, jnp.zeros((LANES - ROPE_DIM,), g_kr.dtype)]).reshape(1, LANES)
    consts = (gmix.reshape(1, d), wa_pad, g_cq.reshape(1, -1), g_ckv.reshape(1, -1), gkr_pad,
              wuq, g_qn.reshape(1, -1), jnp.tile(g_qr[:ROPE_HALF], nh).reshape(1, rw),
              jnp.tile(g_qr[ROPE_HALF:], nh).reshape(1, rw), wuk, g_kn.reshape(1, -1),
              ones_bd, jnp.asarray(perm, BF16))
    ckv, kr, cbf, qcat, kcat = _mla_proj(x, cos, sin, consts)
    tp = b * l
    hw = nh * HEAD_PAD
    o_p = _attn_prompt(qcat, kcat, cbf, b, l)
    c_s = ckv[tp:].reshape(nb, ds, KV_LORA)
    kr_s = kr[tp:].reshape(nb, ds, ROPE_DIM)
    wukt = w_uk.reshape(KV_LORA, nh * NOPE_DIM).T.astype(BF16)
    o_s = _attn_sample(page_table, qcat[tp:].astype(F32).reshape(nb, ds, hw), c_s, kr_s.transpose(0, 2, 1),
                       wukt, g_kn.reshape(1, -1), cache_ckv, cache_krope_t, layer=layer)
    attn = (o_p, o_s, w_uv.transpose(1, 0, 2).astype(BF16), w_o.astype(BF16))
    return (attn, ckv[:tp].reshape(b, l, KV_LORA), kr[:tp].reshape(b, l, ROPE_DIM), c_s, kr_s)


def kernel(x_prompt, x_sample, cache_ckv, cache_krope, state_s5_re, state_s5_im, page_table, norm_ffn1, ffn1_w_gate, ffn1_w_up, ffn1_w_down, norm_mix, norm_ffn2, ffn2_w_gate, ffn2_w_up, ffn2_w_down, ab_w_in, gm_norm_g, gm_norm_b, gm_w_s, gm_b_s, s5_lambda_re, s5_lambda_im, s5_log_dt, s5_b_re, s5_b_im, s5_c_re, s5_c_im, s5_d, s5_w_glu, s5_b_glu, ab_w_out, mla_w_a, mla_g_cq, mla_g_ckv, mla_w_uq, mla_g_qn, mla_g_qr, mla_g_kr, mla_w_uk, mla_g_kn, mla_w_uv, mla_w_o):
    b, l, d = x_prompt.shape
    nb, ds, _ = x_sample.shape
    depth = norm_ffn1.shape[0]
    past = page_table.shape[1] * cache_ckv.shape[2]
    x = (x_prompt.reshape(b * l, d), x_sample.reshape(nb * ds, d))
    cos, sin = _rope_tables(b, l, nb, ds, past)
    cache_krope_t = jnp.swapaxes(cache_krope, 2, 3)
    ckv_p, kr_p, ckv_s, kr_s = [], [], [], []
    s5r_p, s5i_p, s5r_s, s5i_s, gmv_s = [], [], [], [], []
    def ffn1(layer):
        return (norm_ffn1[layer], ffn1_w_gate[layer].astype(BF16), ffn1_w_up[layer].astype(BF16),
                ffn1_w_down[layer].astype(BF16))

    def ffn2(layer):
        return (norm_ffn2[layer], ffn2_w_gate[layer].astype(BF16), ffn2_w_up[layer].astype(BF16),
                ffn2_w_down[layer].astype(BF16))

    pending = []
    for layer in range(depth):
        x = _ffn(x, pending + [ffn1(layer)])
        pending = []
        if layer % 2 == 0:
            i = layer // 2
            x, v_s, hr_p, hi_p, hr_s, hi_s = _even_layer(
                x, b, l, nb, ds, state_s5_re[i], state_s5_im[i], norm_mix[layer], ab_w_in[i],
                gm_norm_g[i], gm_norm_b[i], gm_w_s[i], gm_b_s[i], s5_lambda_re[i], s5_lambda_im[i],
                s5_log_dt[i], s5_b_re[i], s5_b_im[i], s5_c_re[i], s5_c_im[i], s5_d[i], s5_w_glu[i],
                s5_b_glu[i], ab_w_out[i])
            s5r_p.append(hr_p)
            s5i_p.append(hi_p)
            s5r_s.append(hr_s)
            s5i_s.append(hi_s)
            gmv_s.append(v_s)
            pending = [ffn2(layer)]
        else:
            j = layer // 2
            attn, c_p, k_p, c_s, k_s = _mla_layer(
                x, b, l, nb, ds, j, cos, sin, norm_mix[layer], cache_ckv, cache_krope_t, page_table,
                mla_w_a[j], mla_g_cq[j], mla_g_ckv[j], mla_w_uq[j], mla_g_qn[j], mla_g_qr[j],
                mla_g_kr[j], mla_w_uk[j], mla_g_kn[j], mla_w_uv[j], mla_w_o[j])
            ckv_p.append(c_p)
            kr_p.append(k_p)
            ckv_s.append(c_s)
            kr_s.append(k_s)
            x = _ffn(x, [ffn2(layer)], attn=attn, split_rows=b * l if layer == depth - 1 else None)
    if pending:
        x = _ffn(x, pending, split_rows=b * l)
    xp = x[0].reshape(b, l, d)
    xs = x[1].reshape(nb, ds, d)
    return (xp, xs,
            jnp.stack(ckv_p), jnp.stack(kr_p), jnp.stack(ckv_s), jnp.stack(kr_s),
            jnp.stack(s5r_p), jnp.stack(s5i_p), jnp.stack(s5r_s), jnp.stack(s5i_s),
            jnp.stack(gmv_s))
```
